```python
import math
import jax, jax.numpy as jnp
from jax import lax
import numpy as np

D_MODEL = 1024
BATCH = 16
SEQ = 2048
DEPTH = 2

HEAD_DIM = 64
D_MIX = D_MODEL
DIFF_WIDTH = D_MIX // 2
DIFF_HEADS = DIFF_WIDTH // (2 * HEAD_DIM)
SWA_WIDTH = D_MIX - DIFF_WIDTH
SWA_HEADS = SWA_WIDTH // HEAD_DIM
SWA_KV_HEADS = 2
SWA_GROUP = SWA_HEADS // SWA_KV_HEADS
WINDOW = 128
BLOCK = 128
ROPE_THETA = 10000.0
D_FF = -(-8 * D_MODEL // (3 * 256)) * 256
EPS = 1e-6
NEG = -1e30

DQ_DIFF = DIFF_HEADS * 2 * HEAD_DIM
DK_DIFF = DIFF_HEADS * 2 * HEAD_DIM
DV_DIFF = DIFF_HEADS * 2 * HEAD_DIM
DQ_SWA = SWA_HEADS * HEAD_DIM
DK_SWA = SWA_KV_HEADS * HEAD_DIM
DV_SWA = SWA_KV_HEADS * HEAD_DIM
SPLITS = tuple(np.cumsum([DQ_DIFF, DK_DIFF, DV_DIFF, DQ_SWA, DK_SWA])[:].tolist())
D_IN_PROJ = DQ_DIFF + DK_DIFF + DV_DIFF + DQ_SWA + DK_SWA + DV_SWA

kernel_name = "hymba_diffattn_swa_sink_sandwich_adaln"


def rmsnorm(x, g):
    xf = x.astype(jnp.float32)
    xf = xf * lax.rsqrt(jnp.mean(xf * xf, axis=-1, keepdims=True) + EPS)
    return (xf * g.astype(jnp.float32)).astype(x.dtype)


def rope_tables(seq):
    pos = jnp.arange(seq, dtype=jnp.float32)
    inv = ROPE_THETA ** (-jnp.arange(0, HEAD_DIM, 2, dtype=jnp.float32) / HEAD_DIM)
    ang = pos[:, None] * inv[None, :]
    return jnp.cos(ang), jnp.sin(ang)


def apply_rope(x, cos, sin):
    xf = x.astype(jnp.float32)
    x1, x2 = xf[..., : HEAD_DIM // 2], xf[..., HEAD_DIM // 2:]
    c, s = cos[None, :, None, :], sin[None, :, None, :]
    return jnp.concatenate([x1 * c - x2 * s, x2 * c + x1 * s], axis=-1).astype(x.dtype)


def diff_attention(q, k, v, lam, lam_init, subln_g, cos, sin):
    B, S = q.shape[0], q.shape[1]
    q = apply_rope(q.reshape(B, S, DIFF_HEADS * 2, HEAD_DIM), cos, sin).reshape(B, S, DIFF_HEADS, 2, HEAD_DIM)
    k = apply_rope(k.reshape(B, S, DIFF_HEADS * 2, HEAD_DIM), cos, sin).reshape(B, S, DIFF_HEADS, 2, HEAD_DIM)
    v = v.reshape(B, S, DIFF_HEADS, 2 * HEAD_DIM)
    scale = HEAD_DIM ** -0.5
    outs = []
    for i in range(S // BLOCK):
        lo, hi = i * BLOCK, (i + 1) * BLOCK
        s = jnp.einsum('bqhmd,bkhmd->bhmqk', q[:, lo:hi], k[:, :hi]).astype(jnp.float32) * scale
        mask = jnp.arange(hi)[None, :] <= jnp.arange(lo, hi)[:, None]
        p = jax.nn.softmax(jnp.where(mask, s, NEG), axis=-1)
        a = (p[:, :, 0] - lam * p[:, :, 1]).astype(v.dtype)
        outs.append(jnp.einsum('bhqk,bkhe->bqhe', a, v[:, :hi]))
    o = jnp.concatenate(outs, axis=1)
    o = rmsnorm(o, subln_g) * (1.0 - lam_init)
    return o.reshape(B, S, DIFF_WIDTH)


def swa_sink_attention(q, k, v, sinks, cos, sin):
    B, S = q.shape[0], q.shape[1]
    nb = S // BLOCK
    q = apply_rope(q.reshape(B, S, SWA_HEADS, HEAD_DIM), cos, sin)
    q = q.reshape(B, nb, BLOCK, SWA_KV_HEADS, SWA_GROUP, HEAD_DIM)
    k = apply_rope(k.reshape(B, S, SWA_KV_HEADS, HEAD_DIM), cos, sin)
    v = v.reshape(B, S, SWA_KV_HEADS, HEAD_DIM)

    def band(t):
        tp = jnp.pad(t, ((0, 0), (BLOCK, 0), (0, 0), (0, 0))).reshape(B, nb + 1, BLOCK, SWA_KV_HEADS, HEAD_DIM)
        return jnp.concatenate([tp[:, :-1], tp[:, 1:]], axis=2)

    kw, vw = band(k), band(v)
    s = jnp.einsum('bnqhgd,bnkhd->bnhgqk', q, kw).astype(jnp.float32) * (HEAD_DIM ** -0.5)
    r = jnp.arange(BLOCK)[:, None]
    j = jnp.arange(2 * BLOCK)[None, :]
    dist = BLOCK + r - j
    kpos = jnp.arange(nb)[:, None, None] * BLOCK - BLOCK + j[None]
    mask = (dist >= 0)[None] & (dist < WINDOW)[None] & (kpos >= 0)
    s = jnp.where(mask[None, :, None, None], s, NEG)
    sink = jnp.broadcast_to(sinks.astype(jnp.float32).reshape(1, 1, SWA_KV_HEADS, SWA_GROUP, 1, 1),
                            s.shape[:-1] + (1,))
    p = jax.nn.softmax(jnp.concatenate([s, sink], axis=-1), axis=-1)[..., :-1].astype(v.dtype)
    o = jnp.einsum('bnhgqk,bnkhd->bnqhgd', p, vw)
    return o.reshape(B, S, SWA_WIDTH)


def token_mixer(h, w_in, w_out, lam, lam_init, subln_g, sinks, cos, sin):
    proj = jnp.einsum('bsd,de->bse', h, w_in)
    qd, kd, vd, qs, ks, vs = jnp.split(proj, SPLITS, axis=-1)
    o_diff = diff_attention(qd, kd, vd, lam, lam_init, subln_g, cos, sin)
    o_swa = swa_sink_attention(qs, ks, vs, sinks, cos, sin)
    o = jnp.concatenate([o_diff, o_swa], axis=-1)
    return jnp.einsum('bse,ed->bsd', o, w_out)


def swiglu(h, w_gate, w_up, w_down):
    g = jnp.einsum('bsd,df->bsf', h, w_gate)
    u = jnp.einsum('bsd,df->bsf', h, w_up)
    return jnp.einsum('bsf,fd->bsd', jax.nn.silu(g) * u, w_down)


def setup_inputs(seed: int = 0) -> dict:
    key = jax.random.key(seed)
    ks = jax.random.split(key, 20)
    f32 = jnp.float32
    nrm = lambda k, shape, s: jax.random.normal(k, shape, f32) * s
    gain = lambda k, shape: 1.0 + 0.02 * jax.random.normal(k, shape, f32)
    return {
        "x": nrm(ks[0], (BATCH, SEQ, D_MODEL), 1.0),
        "c": nrm(ks[1], (BATCH, D_MODEL), 1.0),
        "ada_w": nrm(ks[2], (DEPTH, D_MODEL, 6 * D_MODEL), 0.5 * D_MODEL ** -0.5),
        "ada_b": nrm(ks[3], (DEPTH, 6 * D_MODEL), 0.02),
        "g_mix_pre": gain(ks[4], (DEPTH, D_MODEL)),
        "g_mix_post": gain(ks[5], (DEPTH, D_MODEL)),
        "g_ffn_pre": gain(ks[6], (DEPTH, D_MODEL)),
        "g_ffn_post": gain(ks[7], (DEPTH, D_MODEL)),
        "w_in": nrm(ks[8], (DEPTH, D_MODEL, D_IN_PROJ), D_MODEL ** -0.5),
        "lambda_q1": nrm(ks[9], (DEPTH, HEAD_DIM), 0.1),
        "lambda_k1": nrm(ks[10], (DEPTH, HEAD_DIM), 0.1),
        "lambda_q2": nrm(ks[11], (DEPTH, HEAD_DIM), 0.1),
        "lambda_k2": nrm(ks[12], (DEPTH, HEAD_DIM), 0.1),
        "subln_g": gain(ks[13], (DEPTH, 2 * HEAD_DIM)),
        "sinks": nrm(ks[14], (DEPTH, SWA_HEADS), 0.5),
        "w_out": nrm(ks[15], (DEPTH, D_MIX, D_MODEL), D_MIX ** -0.5),
        "w_gate": nrm(ks[16], (DEPTH, D_MODEL, D_FF), D_MODEL ** -0.5),
        "w_up": nrm(ks[17], (DEPTH, D_MODEL, D_FF), D_MODEL ** -0.5),
        "w_down": nrm(ks[18], (DEPTH, D_FF, D_MODEL), D_FF ** -0.5),
    }


def reference(x, c, ada_w, ada_b, g_mix_pre, g_mix_post, g_ffn_pre, g_ffn_post, w_in,
              lambda_q1, lambda_k1, lambda_q2, lambda_k2, subln_g, sinks, w_out,
              w_gate, w_up, w_down):
    S = x.shape[1]
    cos, sin = rope_tables(S)
    c_act = jax.nn.silu(c)
    for l in range(DEPTH):
        mod = jnp.einsum('bd,de->be', c_act, ada_w[l]) + ada_b[l]
        sh1, sc1, gt1, sh2, sc2, gt2 = [m[:, None, :] for m in jnp.split(mod, 6, axis=-1)]
        lam_init = 0.8 - 0.6 * math.exp(-0.3 * l)
        lam = (jnp.exp(jnp.sum(lambda_q1[l].astype(jnp.float32) * lambda_k1[l].astype(jnp.float32)))
               - jnp.exp(jnp.sum(lambda_q2[l].astype(jnp.float32) * lambda_k2[l].astype(jnp.float32)))
               + lam_init)
        h = rmsnorm(x, g_mix_pre[l]) * (1.0 + sc1) + sh1
        y = token_mixer(h, w_in[l], w_out[l], lam, lam_init, subln_g[l], sinks[l], cos, sin)
        x = x + gt1 * rmsnorm(y, g_mix_post[l])
        h = rmsnorm(x, g_ffn_pre[l]) * (1.0 + sc2) + sh2
        y = swiglu(h, w_gate[l], w_up[l], w_down[l])
        x = x + gt2 * rmsnorm(y, g_ffn_post[l])
    return x
```

```python
import functools
import math

import numpy as np
import jax
import jax.numpy as jnp
from jax import lax
from jax.experimental import pallas as pl
from jax.experimental.pallas import tpu as pltpu

D_MODEL = 1024
BATCH = 16
SEQ = 2048
DEPTH = 2
HEAD_DIM = 64
HALF = HEAD_DIM // 2
DIFF_HEADS = 4
DIFF_WIDTH = 512
SWA_HEADS = 8
SWA_KV_HEADS = 2
SWA_GROUP = SWA_HEADS // SWA_KV_HEADS
SWA_WIDTH = 512
WINDOW = 128
ROPE_THETA = 10000.0
D_FF = 2816
D_IN_PROJ = 2304
EPS = 1e-6
NEG = -1e30

LANES = 128
Q_DIFF, K_DIFF, V_DIFF, Q_SWA, K_SWA, V_SWA = 0, 512, 1024, 1536, 2048, 2176
ROPE_GROUPS = tuple(range(0, 8)) + tuple(range(12, 17))
N_GROUPS = D_IN_PROJ // LANES

VMEM_LIMIT_BYTES = 56 * 1024 * 1024

TM_PRE = 512
TM_POST = 256
TQ_DIFF = 256
TQ_SWA = 512

F32 = jnp.float32
BF16 = jnp.bfloat16


def _in_proj_layout():
    perm = np.arange(D_IN_PROJ)
    scale = np.ones(D_IN_PROJ, np.float32)
    j = np.arange(LANES)
    second, which, d = j // 64, (j // HALF) % 2, j % HALF
    for h in range(DIFF_HEADS):
        within = which * HEAD_DIM + second * HALF + d
        perm[Q_DIFF + h * LANES + j] = Q_DIFF + h * LANES + within
        perm[K_DIFF + h * LANES + j] = K_DIFF + h * LANES + within
    for g in range(SWA_GROUP):
        perm[Q_SWA + g * LANES + j] = Q_SWA + (which * SWA_GROUP + g) * HEAD_DIM + second * HALF + d
    perm[K_SWA + j] = K_SWA + which * HEAD_DIM + second * HALF + d
    score_scale = HEAD_DIM ** -0.5
    scale[Q_DIFF:Q_DIFF + 512] = score_scale
    scale[Q_SWA:Q_SWA + 512] = score_scale
    return perm, scale


def _out_proj_row_order():
    perm = np.arange(D_MODEL)
    d = np.arange(HEAD_DIM)
    for g in range(SWA_GROUP):
        for t in range(SWA_KV_HEADS):
            perm[DIFF_WIDTH + g * LANES + t * HEAD_DIM + d] = DIFF_WIDTH + (t * SWA_GROUP + g) * HEAD_DIM + d
    return perm


def _rope_tables():
    pos = jnp.arange(SEQ, dtype=F32)
    inv = ROPE_THETA ** (-jnp.arange(0, HEAD_DIM, 2, dtype=F32) / HEAD_DIM)
    ang = pos[:, None] * inv[None, :]
    cos, sin = jnp.cos(ang), jnp.sin(ang)
    cos4 = jnp.concatenate([cos, cos, cos, cos], axis=-1)
    sin4 = jnp.concatenate([-sin, -sin, sin, sin], axis=-1)
    return cos4, sin4


def _rms(x):
    return x * lax.rsqrt(jnp.mean(x * x, axis=-1, keepdims=True) + EPS)


def _ada_kernel(c_ref, w_ref, b_ref, o_ref):
    c = c_ref[...]
    c_act = c / (1.0 + jnp.exp(-c))
    o_ref[0] = jnp.dot(c_act, w_ref[0], preferred_element_type=F32) + b_ref[0]


def _ada_call(c, ada_w, ada_b):
    n_chunk = 6
    return pl.pallas_call(
        _ada_kernel,
        grid=(DEPTH, n_chunk),
        in_specs=[
            pl.BlockSpec((BATCH, D_MODEL), lambda l, k: (0, 0)),
            pl.BlockSpec((1, D_MODEL, D_MODEL), lambda l, k: (l, 0, k)),
            pl.BlockSpec((1, 1, D_MODEL), lambda l, k: (l, 0, k)),
        ],
        out_specs=pl.BlockSpec((1, BATCH, D_MODEL), lambda l, k: (l, 0, k)),
        out_shape=jax.ShapeDtypeStruct((DEPTH, BATCH, 6 * D_MODEL), F32),
        name="ada_mod",
    )(c, ada_w, ada_b.reshape(DEPTH, 1, 6 * D_MODEL))


def _pre_kernel(x_ref, g_ref, sh_ref, sc_ref, w_ref, cos_ref, sin_ref, o_ref):
    x = x_ref[0]
    h = (_rms(x) * g_ref[...]) * (1.0 + sc_ref[0]) + sh_ref[0]
    proj = jnp.dot(h.astype(BF16), w_ref[...], preferred_element_type=F32)
    cos = cos_ref[...]
    sin = sin_ref[...]
    for grp in range(N_GROUPS):
        cols = slice(grp * LANES, (grp + 1) * LANES)
        xg = proj[:, cols]
        if grp in ROPE_GROUPS:
            xg = xg * cos + pltpu.roll(xg, 64, 1) * sin
        o_ref[0, :, cols] = xg.astype(BF16)


def _mod_spec(layer, which, n_inner):
    def index(b, i):
        return ((layer * BATCH + b) * 6 + which, 0, 0)
    return pl.BlockSpec((1, 1, D_MODEL), index)


def _pre_call(layer, x, mod3, g_pre, w_in_b, cos4, sin4):
    n_t = SEQ // TM_PRE
    return pl.pallas_call(
        _pre_kernel,
        grid=(BATCH, n_t),
        in_specs=[
            pl.BlockSpec((1, TM_PRE, D_MODEL), lambda b, i: (b, i, 0)),
            pl.BlockSpec((1, D_MODEL), lambda b, i: (0, 0)),
            _mod_spec(layer, 0, n_t),
            _mod_spec(layer, 1, n_t),
            pl.BlockSpec((D_MODEL, D_IN_PROJ), lambda b, i: (0, 0)),
            pl.BlockSpec((TM_PRE, LANES), lambda b, i: (i, 0)),
            pl.BlockSpec((TM_PRE, LANES), lambda b, i: (i, 0)),
        ],
        out_specs=pl.BlockSpec((1, TM_PRE, D_IN_PROJ), lambda b, i: (b, i, 0)),
        out_shape=jax.ShapeDtypeStruct((BATCH, SEQ, D_IN_PROJ), BF16),
        compiler_params=pltpu.CompilerParams(
            dimension_semantics=("arbitrary", "arbitrary"),
            vmem_limit_bytes=VMEM_LIMIT_BYTES),
        name="pre_mixer",
    )(x, g_pre.reshape(1, D_MODEL), mod3, mod3, w_in_b, cos4, sin4)


def _diff_kernel(lam_ref, g_ref, q_ref, k_ref, v_ref, o_ref, m_ref, l_ref, acc_ref, *, lam_init):
    i = pl.program_id(1)
    tq = TQ_DIFF
    lane = lax.broadcasted_iota(jnp.int32, (1, LANES), 1)
    first_map = ((lane // HALF) % 2) == 0
    nt_dims = (((1,), (1,)), ((), ()))

    q_maps = []
    for h in range(DIFF_HEADS):
        qh = q_ref[0, :, h * LANES:(h + 1) * LANES]
        zero = jnp.zeros_like(qh)
        q_maps.append(jnp.where(first_map, qh, zero))
        q_maps.append(jnp.where(first_map, zero, qh))

    def attend(j, diagonal):
        start = pl.multiple_of(j * tq, tq)
        for h in range(DIFF_HEADS):
            kh = k_ref[0, pl.ds(start, tq), h * LANES:(h + 1) * LANES]
            vh = v_ref[0, pl.ds(start, tq), h * LANES:(h + 1) * LANES]
            for mp in range(2):
                idx = 2 * h + mp
                s = lax.dot_general(q_maps[idx], kh, nt_dims, preferred_element_type=F32)
                if diagonal:
                    r = lax.broadcasted_iota(jnp.int32, (tq, tq), 0)
                    c = lax.broadcasted_iota(jnp.int32, (tq, tq), 1)
                    s = jnp.where(c <= r, s, NEG)
                    m_new = jnp.max(s, axis=-1, keepdims=True)
                    p = jnp.exp(s - m_new)
                    l_ref[idx] = jnp.sum(p, axis=-1, keepdims=True)
                    acc_ref[idx] = jnp.dot(p.astype(BF16), vh, preferred_element_type=F32)
                else:
                    m_old = m_ref[idx]
                    m_new = jnp.maximum(m_old, jnp.max(s, axis=-1, keepdims=True))
                    p = jnp.exp(s - m_new)
                    alpha = jnp.exp(m_old - m_new)
                    l_ref[idx] = alpha * l_ref[idx] + jnp.sum(p, axis=-1, keepdims=True)
                    acc_ref[idx] = alpha * acc_ref[idx] + jnp.dot(
                        p.astype(BF16), vh, preferred_element_type=F32)
                m_ref[idx] = m_new

    attend(i, True)

    def body(j, carry):
        attend(j, False)
        return carry

    lax.fori_loop(0, i, body, 0)

    lam_vec = lam_ref[...]
    dot1 = jnp.sum(lam_vec[0:1] * lam_vec[1:2], axis=-1, keepdims=True)
    dot2 = jnp.sum(lam_vec[2:3] * lam_vec[3:4], axis=-1, keepdims=True)
    lam = jnp.exp(dot1) - jnp.exp(dot2) + lam_init
    g = g_ref[...]
    for h in range(DIFF_HEADS):
        o1 = acc_ref[2 * h] / l_ref[2 * h]
        o2 = acc_ref[2 * h + 1] / l_ref[2 * h + 1]
        o = o1 - lam * o2
        o = (_rms(o) * g) * (1.0 - lam_init)
        o_ref[0, :, h * LANES:(h + 1) * LANES] = o.astype(BF16)


def _diff_call(layer, proj, lam_vec, subln_g):
    lam_init = 0.8 - 0.6 * math.exp(-0.3 * layer)
    n_q = SEQ // TQ_DIFF
    kernel = functools.partial(_diff_kernel, lam_init=lam_init)
    return pl.pallas_call(
        kernel,
        grid=(BATCH, n_q),
        in_specs=[
            pl.BlockSpec((4, HEAD_DIM), lambda b, i: (0, 0)),
            pl.BlockSpec((1, LANES), lambda b, i: (0, 0)),
            pl.BlockSpec((1, TQ_DIFF, DIFF_WIDTH), lambda b, i: (b, i, Q_DIFF // 512)),
            pl.BlockSpec((1, SEQ, DIFF_WIDTH), lambda b, i: (b, 0, K_DIFF // 512)),
            pl.BlockSpec((1, SEQ, DIFF_WIDTH), lambda b, i: (b, 0, V_DIFF // 512)),
        ],
        out_specs=pl.BlockSpec((1, TQ_DIFF, DIFF_WIDTH), lambda b, i: (b, i, 0)),
        out_shape=jax.ShapeDtypeStruct((BATCH, SEQ, DIFF_WIDTH), BF16),
        scratch_shapes=[
            pltpu.VMEM((2 * DIFF_HEADS, TQ_DIFF, 1), F32),
            pltpu.VMEM((2 * DIFF_HEADS, TQ_DIFF, 1), F32),
            pltpu.VMEM((2 * DIFF_HEADS, TQ_DIFF, LANES), F32),
        ],
        compiler_params=pltpu.CompilerParams(
            dimension_semantics=("arbitrary", "arbitrary"),
            vmem_limit_bytes=VMEM_LIMIT_BYTES),
        name="diff_attn",
    )(lam_vec, subln_g.reshape(1, LANES), proj, proj, proj)


def _swa_kernel(sink_ref, q_ref, k_ref, v_ref, o_ref):
    i = pl.program_id(1)
    n_sub = TQ_SWA // WINDOW
    band = 2 * WINDOW
    lane = lax.broadcasted_iota(jnp.int32, (1, LANES), 1)
    kv0_lanes = ((lane // HALF) % 2) == 0
    nt_dims = (((1,), (1,)), ((), ()))
    r = lax.broadcasted_iota(jnp.int32, (WINDOW, band), 0)
    c = lax.broadcasted_iota(jnp.int32, (WINDOW, band), 1)
    for n in range(n_sub):
        blk = i * n_sub + n
        kstart = pl.multiple_of(jnp.maximum(blk - 1, 0) * WINDOW, WINDOW)
        kb = k_ref[0, pl.ds(kstart, band), :]
        vb = v_ref[0, pl.ds(kstart, band), :]
        dist = (blk * WINDOW - kstart) + r - c
        valid = (dist >= 0) & (dist < WINDOW)
        for g in range(SWA_GROUP):
            qg = q_ref[0, n * WINDOW:(n + 1) * WINDOW, g * LANES:(g + 1) * LANES]
            zero = jnp.zeros_like(qg)
            outs = []
            for t in range(SWA_KV_HEADS):
                qt = jnp.where(kv0_lanes, qg, zero) if t == 0 else jnp.where(kv0_lanes, zero, qg)
                sink = sink_ref[t * SWA_GROUP + g]
                s = lax.dot_general(qt, kb, nt_dims, preferred_element_type=F32)
                s = jnp.where(valid, s, NEG)
                m = jnp.maximum(jnp.max(s, axis=-1, keepdims=True), sink)
                p = jnp.exp(s - m)
                denom = jnp.sum(p, axis=-1, keepdims=True) + jnp.exp(sink - m)
                outs.append(jnp.dot(p.astype(BF16), vb, preferred_element_type=F32) / denom)
            og = jnp.where(lane < HEAD_DIM, outs[0], outs[1])
            o_ref[0, n * WINDOW:(n + 1) * WINDOW, g * LANES:(g + 1) * LANES] = og.astype(BF16)


def _swa_call(proj, sinks):
    n_q = SEQ // TQ_SWA
    return pl.pallas_call(
        _swa_kernel,
        grid=(BATCH, n_q),
        in_specs=[
            pl.BlockSpec(memory_space=pltpu.SMEM),
            pl.BlockSpec((1, TQ_SWA, SWA_WIDTH), lambda b, i: (b, i, Q_SWA // 512)),
            pl.BlockSpec((1, SEQ, LANES), lambda b, i: (b, 0, K_SWA // LANES)),
            pl.BlockSpec((1, SEQ, LANES), lambda b, i: (b, 0, V_SWA // LANES)),
        ],
        out_specs=pl.BlockSpec((1, TQ_SWA, SWA_WIDTH), lambda b, i: (b, i, 0)),
        out_shape=jax.ShapeDtypeStruct((BATCH, SEQ, SWA_WIDTH), BF16),
        compiler_params=pltpu.CompilerParams(
            dimension_semantics=("arbitrary", "arbitrary"),
            vmem_limit_bytes=VMEM_LIMIT_BYTES),
        name="swa_attn",
    )(sinks, proj, proj, proj)


def _post_kernel(x_ref, od_ref, os_ref, wo_ref, gpost1_ref, gt1_ref, gpre2_ref, sc2_ref, sh2_ref,
                 wg_ref, wu_ref, wd_ref, gpost2_ref, gt2_ref, o_ref):
    x = x_ref[0]
    y = jnp.dot(od_ref[0], wo_ref[0:DIFF_WIDTH, :], preferred_element_type=F32)
    y = y + jnp.dot(os_ref[0], wo_ref[DIFF_WIDTH:D_MODEL, :], preferred_element_type=F32)
    x1 = x + gt1_ref[0] * (_rms(y) * gpost1_ref[...])
    h = (_rms(x1) * gpre2_ref[...]) * (1.0 + sc2_ref[0]) + sh2_ref[0]
    hb = h.astype(BF16)
    gate = jnp.dot(hb, wg_ref[...], preferred_element_type=F32)
    up = jnp.dot(hb, wu_ref[...], preferred_element_type=F32)
    act = (gate / (1.0 + jnp.exp(-gate))) * up
    y2 = jnp.dot(act.astype(BF16), wd_ref[...], preferred_element_type=F32)
    o_ref[0] = x1 + gt2_ref[0] * (_rms(y2) * gpost2_ref[...])


def _resident(shape):
    return pl.BlockSpec(shape, lambda b, i: (0,) * len(shape), pipeline_mode=pl.Buffered(1))


def _post_call(layer, x, o_diff, o_swa, mod3, w_out_b, g_post1, g_pre2, w_gate_b, w_up_b, w_down_b, g_post2):
    n_t = SEQ // TM_POST
    row = lambda: pl.BlockSpec((1, D_MODEL), lambda b, i: (0, 0))
    return pl.pallas_call(
        _post_kernel,
        grid=(BATCH, n_t),
        in_specs=[
            pl.BlockSpec((1, TM_POST, D_MODEL), lambda b, i: (b, i, 0)),
            pl.BlockSpec((1, TM_POST, DIFF_WIDTH), lambda b, i: (b, i, 0)),
            pl.BlockSpec((1, TM_POST, SWA_WIDTH), lambda b, i: (b, i, 0)),
            _resident((D_MODEL, D_MODEL)),
            row(),
            _mod_spec(layer, 2, n_t),
            row(),
            _mod_spec(layer, 4, n_t),
            _mod_spec(layer, 3, n_t),
            _resident((D_MODEL, D_FF)),
            _resident((D_MODEL, D_FF)),
            _resident((D_FF, D_MODEL)),
            row(),
            _mod_spec(layer, 5, n_t),
        ],
        out_specs=pl.BlockSpec((1, TM_POST, D_MODEL), lambda b, i: (b, i, 0)),
        out_shape=jax.ShapeDtypeStruct((BATCH, SEQ, D_MODEL), F32),
        compiler_params=pltpu.CompilerParams(
            dimension_semantics=("arbitrary", "arbitrary"),
            vmem_limit_bytes=VMEM_LIMIT_BYTES),
        name="post_ffn",
    )(x, o_diff, o_swa, w_out_b, g_post1.reshape(1, D_MODEL), mod3, g_pre2.reshape(1, D_MODEL),
      mod3, mod3, w_gate_b, w_up_b, w_down_b, g_post2.reshape(1, D_MODEL), mod3)


def kernel(x, c, ada_w, ada_b, g_mix_pre, g_mix_post, g_ffn_pre, g_ffn_post, w_in, lambda_q1, lambda_k1,
           lambda_q2, lambda_k2, subln_g, sinks, w_out, w_gate, w_up, w_down):
    in_perm, in_scale = _in_proj_layout()
    out_perm = _out_proj_row_order()
    cos4, sin4 = _rope_tables()

    mod = _ada_call(c, ada_w, ada_b)
    mod3 = mod.reshape(DEPTH * BATCH * 6, 1, D_MODEL)

    for layer in range(DEPTH):
        w_in_b = (w_in[layer][:, in_perm] * in_scale[None, :]).astype(BF16)
        w_out_b = w_out[layer][out_perm, :].astype(BF16)
        lam_vec = jnp.stack([lambda_q1[layer], lambda_k1[layer], lambda_q2[layer], lambda_k2[layer]])

        proj = _pre_call(layer, x, mod3, g_mix_pre[layer], w_in_b, cos4, sin4)
        o_diff = _diff_call(layer, proj, lam_vec, subln_g[layer])
        o_swa = _swa_call(proj, sinks[layer])
        x = _post_call(layer, x, o_diff, o_swa, mod3, w_out_b, g_mix_post[layer], g_ffn_pre[layer],
                       w_gate[layer].astype(BF16), w_up[layer].astype(BF16), w_down[layer].astype(BF16),
                       g_ffn_post[layer])
    return x
```

```python
import functools
import math

import numpy as np
import jax
import jax.numpy as jnp
from jax import lax
from jax.experimental import pallas as pl
from jax.experimental.pallas import tpu as pltpu

D_MODEL = 1024
BATCH = 16
SEQ = 2048
DEPTH = 2
HEAD_DIM = 64
HALF = HEAD_DIM // 2
DIFF_HEADS = 4
DIFF_WIDTH = 512
SWA_HEADS = 8
SWA_KV_HEADS = 2
SWA_GROUP = SWA_HEADS // SWA_KV_HEADS
SWA_WIDTH = 512
WINDOW = 128
ROPE_THETA = 10000.0
D_FF = 2816
EPS = 1e-6
NEG = -1e30

LANES = 128

REF_Q_DIFF, REF_K_DIFF, REF_V_DIFF, REF_Q_SWA, REF_K_SWA, REF_V_SWA = 0, 512, 1024, 1536, 2048, 2176
A_K_DIFF, A_Q_SWA, A_K_SWA, A_V_SWA, A_WIDTH = 0, 512, 1024, 1152, 1280
A_ROPE_GROUPS = 9
T_Q_DIFF, T_V_DIFF, T_WIDTH = 0, 512, 1024

VMEM_LIMIT_BYTES = 56 * 1024 * 1024

TM_PRE = 512
TM_POST = 256
T_DIFF = 256
TQ_SWA = 512

F32 = jnp.float32
BF16 = jnp.bfloat16
NT_DIMS = (((1,), (1,)), ((), ()))


def _in_proj_layout():
    j = np.arange(LANES)
    second, which, d = j // 64, (j // HALF) % 2, j % HALF
    pair_order = which * HEAD_DIM + second * HALF + d
    score_scale = HEAD_DIM ** -0.5

    cols_a = np.zeros(A_WIDTH, np.int64)
    scale_a = np.ones(A_WIDTH, np.float32)
    for h in range(DIFF_HEADS):
        cols_a[A_K_DIFF + h * LANES + j] = REF_K_DIFF + h * LANES + pair_order
    for g in range(SWA_GROUP):
        cols_a[A_Q_SWA + g * LANES + j] = REF_Q_SWA + (which * SWA_GROUP + g) * HEAD_DIM + second * HALF + d
    scale_a[A_Q_SWA:A_Q_SWA + SWA_WIDTH] = score_scale
    cols_a[A_K_SWA + j] = REF_K_SWA + pair_order
    cols_a[A_V_SWA + j] = REF_V_SWA + j

    cols_t = np.zeros(T_WIDTH, np.int64)
    scale_t = np.ones(T_WIDTH, np.float32)
    for h in range(DIFF_HEADS):
        cols_t[T_Q_DIFF + h * LANES + j] = REF_Q_DIFF + h * LANES + pair_order
    scale_t[T_Q_DIFF:T_Q_DIFF + DIFF_WIDTH] = score_scale
    cols_t[T_V_DIFF:T_V_DIFF + DIFF_WIDTH] = REF_V_DIFF + np.arange(DIFF_WIDTH)
    return cols_a, scale_a, cols_t, scale_t


def _out_proj_row_order():
    perm = np.arange(D_MODEL)
    d = np.arange(HEAD_DIM)
    for g in range(SWA_GROUP):
        for t in range(SWA_KV_HEADS):
            perm[DIFF_WIDTH + g * LANES + t * HEAD_DIM + d] = DIFF_WIDTH + (t * SWA_GROUP + g) * HEAD_DIM + d
    return perm


def _rope_tables():
    pos = jnp.arange(SEQ, dtype=F32)
    inv = ROPE_THETA ** (-jnp.arange(0, HEAD_DIM, 2, dtype=F32) / HEAD_DIM)
    ang = pos[:, None] * inv[None, :]
    cos, sin = jnp.cos(ang), jnp.sin(ang)
    cos4 = jnp.concatenate([cos, cos, cos, cos], axis=-1)
    sin4 = jnp.concatenate([-sin, -sin, sin, sin], axis=-1)
    cos_t = jnp.concatenate([cos, cos], axis=-1).T
    sin_t = jnp.concatenate([sin, sin], axis=-1).T
    return cos4, sin4, cos_t, sin_t


def _rms(x):
    return x * lax.rsqrt(jnp.mean(x * x, axis=-1, keepdims=True) + EPS)


def _ada_kernel(c_ref, w_ref, b_ref, o_ref):
    c = c_ref[...]
    c_act = c / (1.0 + jnp.exp(-c))
    o_ref[0] = jnp.dot(c_act, w_ref[0], preferred_element_type=F32) + b_ref[0]


def _ada_call(c, ada_w, ada_b):
    n_chunk = 6
    return pl.pallas_call(
        _ada_kernel,
        grid=(DEPTH, n_chunk),
        in_specs=[
            pl.BlockSpec((BATCH, D_MODEL), lambda l, k: (0, 0)),
            pl.BlockSpec((1, D_MODEL, D_MODEL), lambda l, k: (l, 0, k)),
            pl.BlockSpec((1, 1, D_MODEL), lambda l, k: (l, 0, k)),
        ],
        out_specs=pl.BlockSpec((1, BATCH, D_MODEL), lambda l, k: (l, 0, k)),
        out_shape=jax.ShapeDtypeStruct((DEPTH, BATCH, 6 * D_MODEL), F32),
        name="ada_mod",
    )(c, ada_w, ada_b.reshape(DEPTH, 1, 6 * D_MODEL))


def _pre_kernel(x_ref, g_ref, sh_ref, sc_ref, wa_ref, wt_ref, cos_ref, sin_ref, cost_ref, sint_ref,
                oa_ref, ot_ref):
    x = x_ref[0]
    h = (_rms(x) * g_ref[...]) * (1.0 + sc_ref[0]) + sh_ref[0]
    hb = h.astype(BF16)

    proj = jnp.dot(hb, wa_ref[...], preferred_element_type=F32)
    cos = cos_ref[...]
    sin = sin_ref[...]
    for grp in range(A_WIDTH // LANES):
        cols = slice(grp * LANES, (grp + 1) * LANES)
        xg = proj[:, cols]
        if grp < A_ROPE_GROUPS:
            xg = xg * cos + pltpu.roll(xg, 64, 1) * sin
        oa_ref[0, :, cols] = xg.astype(BF16)

    proj_t = lax.dot_general(wt_ref[...], hb, NT_DIMS, preferred_element_type=F32)
    cos_t = cost_ref[...]
    sin_t = sint_ref[...]
    n_blk = TM_PRE // T_DIFF
    for hd in range(DIFF_HEADS):
        lo = proj_t[T_Q_DIFF + hd * LANES:T_Q_DIFF + hd * LANES + 64]
        hi = proj_t[T_Q_DIFF + hd * LANES + 64:T_Q_DIFF + (hd + 1) * LANES]
        out_lo = (lo * cos_t - hi * sin_t).astype(BF16)
        out_hi = (hi * cos_t + lo * sin_t).astype(BF16)
        for blk in range(n_blk):
            tok = slice(blk * T_DIFF, (blk + 1) * T_DIFF)
            ot_ref[0, blk, T_Q_DIFF + hd * LANES:T_Q_DIFF + hd * LANES + 64, :] = out_lo[:, tok]
            ot_ref[0, blk, T_Q_DIFF + hd * LANES + 64:T_Q_DIFF + (hd + 1) * LANES, :] = out_hi[:, tok]
    for blk in range(n_blk):
        tok = slice(blk * T_DIFF, (blk + 1) * T_DIFF)
        ot_ref[0, blk, T_V_DIFF:T_WIDTH, :] = proj_t[T_V_DIFF:T_WIDTH, tok].astype(BF16)


def _mod_spec(layer, which):
    def index(b, i):
        return ((layer * BATCH + b) * 6 + which, 0, 0)
    return pl.BlockSpec((1, 1, D_MODEL), index)


def _resident(shape):
    return pl.BlockSpec(shape, lambda b, i: (0,) * len(shape), pipeline_mode=pl.Buffered(1))


def _pre_call(layer, x, mod3, g_pre, w_a, w_t, tables):
    cos4, sin4, cos_t, sin_t = tables
    n_t = SEQ // TM_PRE
    n_blk = TM_PRE // T_DIFF
    return pl.pallas_call(
        _pre_kernel,
        grid=(BATCH, n_t),
        in_specs=[
            pl.BlockSpec((1, TM_PRE, D_MODEL), lambda b, i: (b, i, 0)),
            pl.BlockSpec((1, D_MODEL), lambda b, i: (0, 0)),
            _mod_spec(layer, 0),
            _mod_spec(layer, 1),
            _resident((D_MODEL, A_WIDTH)),
            _resident((T_WIDTH, D_MODEL)),
            pl.BlockSpec((TM_PRE, LANES), lambda b, i: (i, 0)),
            pl.BlockSpec((TM_PRE, LANES), lambda b, i: (i, 0)),
            pl.BlockSpec((HEAD_DIM, TM_PRE), lambda b, i: (0, i)),
            pl.BlockSpec((HEAD_DIM, TM_PRE), lambda b, i: (0, i)),
        ],
        out_specs=[
            pl.BlockSpec((1, TM_PRE, A_WIDTH), lambda b, i: (b, i, 0)),
            pl.BlockSpec((1, n_blk, T_WIDTH, T_DIFF), lambda b, i: (b, i, 0, 0)),
        ],
        out_shape=[
            jax.ShapeDtypeStruct((BATCH, SEQ, A_WIDTH), BF16),
            jax.ShapeDtypeStruct((BATCH, SEQ // T_DIFF, T_WIDTH, T_DIFF), BF16),
        ],
        compiler_params=pltpu.CompilerParams(
            dimension_semantics=("arbitrary", "arbitrary"),
            vmem_limit_bytes=VMEM_LIMIT_BYTES),
        name="pre_mixer",
    )(x, g_pre.reshape(1, D_MODEL), mod3, mod3, w_a, w_t, cos4, sin4, cos_t, sin_t)


def _diff_kernel(lam_ref, g_ref, qt_ref, k_ref, vt_ref, o_ref, qbd_ref, m_ref, l_ref, acc_ref, *, lam_init):
    i = pl.program_id(1)
    t = T_DIFF
    feat = lax.broadcasted_iota(jnp.int32, (LANES, 1), 0)
    first_map = ((feat // HALF) % 2) == 0

    for h in range(DIFF_HEADS):
        qh = qt_ref[0, 0, h * LANES:(h + 1) * LANES, :]
        zero = jnp.zeros_like(qh)
        qbd_ref[h, :, 0:t] = jnp.where(first_map, qh, zero)
        qbd_ref[h, :, t:2 * t] = jnp.where(first_map, zero, qh)

    def attend(j, diagonal):
        start = pl.multiple_of(j * t, t)
        for h in range(DIFF_HEADS):
            kh = k_ref[0, pl.ds(start, t), h * LANES:(h + 1) * LANES]
            vth = vt_ref[0, j, h * LANES:(h + 1) * LANES, :]
            s = jnp.dot(kh, qbd_ref[h], preferred_element_type=F32)
            if diagonal:
                key = lax.broadcasted_iota(jnp.int32, (t, 2 * t), 0)
                qry = lax.broadcasted_iota(jnp.int32, (t, 2 * t), 1) & (t - 1)
                s = jnp.where(key <= qry, s, NEG)
                m_new = jnp.max(s, axis=0, keepdims=True)
                p = jnp.exp(s - m_new)
                l_ref[h] = jnp.sum(p, axis=0, keepdims=True)
                acc_ref[h] = jnp.dot(vth, p.astype(BF16), preferred_element_type=F32)
            else:
                m_old = m_ref[h]
                m_new = jnp.maximum(m_old, jnp.max(s, axis=0, keepdims=True))
                p = jnp.exp(s - m_new)
                alpha = jnp.exp(m_old - m_new)
                l_ref[h] = alpha * l_ref[h] + jnp.sum(p, axis=0, keepdims=True)
                acc_ref[h] = alpha * acc_ref[h] + jnp.dot(vth, p.astype(BF16), preferred_element_type=F32)
            m_ref[h] = m_new

    attend(i, True)

    def body(j, carry):
        attend(j, False)
        return carry

    lax.fori_loop(0, i, body, 0)

    lam_vec = lam_ref[...]
    dot1 = jnp.sum(lam_vec[0:1] * lam_vec[1:2], axis=-1, keepdims=True)
    dot2 = jnp.sum(lam_vec[2:3] * lam_vec[3:4], axis=-1, keepdims=True)
    lam = jnp.exp(dot1) - jnp.exp(dot2) + lam_init
    g = g_ref[...]
    for h in range(DIFF_HEADS):
        o_all = acc_ref[h] * (1.0 / l_ref[h])
        o = o_all[:, 0:t] - lam * o_all[:, t:2 * t]
        o = o * lax.rsqrt(jnp.mean(o * o, axis=0, keepdims=True) + EPS)
        o = (o * g) * (1.0 - lam_init)
        o_ref[0, :, h * LANES:(h + 1) * LANES] = o.T.astype(BF16)


def _diff_call(layer, proj_a, proj_t, lam_vec, subln_g):
    lam_init = 0.8 - 0.6 * math.exp(-0.3 * layer)
    n_q = SEQ // T_DIFF
    kernel = functools.partial(_diff_kernel, lam_init=lam_init)
    return pl.pallas_call(
        kernel,
        grid=(BATCH, n_q),
        in_specs=[
            pl.BlockSpec((4, HEAD_DIM), lambda b, i: (0, 0)),
            pl.BlockSpec((LANES, 1), lambda b, i: (0, 0)),
            pl.BlockSpec((1, 1, DIFF_WIDTH, T_DIFF), lambda b, i: (b, i, T_Q_DIFF // DIFF_WIDTH, 0)),
            pl.BlockSpec((1, SEQ, DIFF_WIDTH), lambda b, i: (b, 0, A_K_DIFF // DIFF_WIDTH)),
            pl.BlockSpec((1, n_q, DIFF_WIDTH, T_DIFF), lambda b, i: (b, 0, T_V_DIFF // DIFF_WIDTH, 0)),
        ],
        out_specs=pl.BlockSpec((1, T_DIFF, DIFF_WIDTH), lambda b, i: (b, i, 0)),
        out_shape=jax.ShapeDtypeStruct((BATCH, SEQ, DIFF_WIDTH), BF16),
        scratch_shapes=[
            pltpu.VMEM((DIFF_HEADS, LANES, 2 * T_DIFF), BF16),
            pltpu.VMEM((DIFF_HEADS, 1, 2 * T_DIFF), F32),
            pltpu.VMEM((DIFF_HEADS, 1, 2 * T_DIFF), F32),
            pltpu.VMEM((DIFF_HEADS, LANES, 2 * T_DIFF), F32),
        ],
        compiler_params=pltpu.CompilerParams(
            dimension_semantics=("arbitrary", "arbitrary"),
            vmem_limit_bytes=VMEM_LIMIT_BYTES),
        name="diff_attn",
    )(lam_vec, subln_g.reshape(LANES, 1), proj_t, proj_a, proj_t)


def _swa_kernel(sink_ref, q_ref, k_ref, v_ref, o_ref):
    i = pl.program_id(1)
    n_sub = TQ_SWA // WINDOW
    band = 2 * WINDOW
    lane = lax.broadcasted_iota(jnp.int32, (1, LANES), 1)
    kv0_lanes = ((lane // HALF) % 2) == 0
    r = lax.broadcasted_iota(jnp.int32, (WINDOW, band), 0)
    c = lax.broadcasted_iota(jnp.int32, (WINDOW, band), 1)
    for n in range(n_sub):
        blk = i * n_sub + n
        kstart = pl.multiple_of(jnp.maximum(blk - 1, 0) * WINDOW, WINDOW)
        kb = k_ref[0, pl.ds(kstart, band), :]
        vb = v_ref[0, pl.ds(kstart, band), :]
        dist = (blk * WINDOW - kstart) + r - c
        valid = (dist >= 0) & (dist < WINDOW)
        for g in range(SWA_GROUP):
            qg = q_ref[0, n * WINDOW:(n + 1) * WINDOW, g * LANES:(g + 1) * LANES]
            zero = jnp.zeros_like(qg)
            outs = []
            for kvh in range(SWA_KV_HEADS):
                qt = jnp.where(kv0_lanes, qg, zero) if kvh == 0 else jnp.where(kv0_lanes, zero, qg)
                sink = sink_ref[kvh * SWA_GROUP + g]
                s = lax.dot_general(qt, kb, NT_DIMS, preferred_element_type=F32)
                s = jnp.where(valid, s, NEG)
                m = jnp.maximum(jnp.max(s, axis=-1, keepdims=True), sink)
                p = jnp.exp(s - m)
                denom = jnp.sum(p, axis=-1, keepdims=True) + jnp.exp(sink - m)
                outs.append(jnp.dot(p.astype(BF16), vb, preferred_element_type=F32) / denom)
            og = jnp.where(lane < HEAD_DIM, outs[0], outs[1])
            o_ref[0, n * WINDOW:(n + 1) * WINDOW, g * LANES:(g + 1) * LANES] = og.astype(BF16)


def _swa_call(proj_a, sinks):
    n_q = SEQ // TQ_SWA
    return pl.pallas_call(
        _swa_kernel,
        grid=(BATCH, n_q),
        in_specs=[
            pl.BlockSpec(memory_space=pltpu.SMEM),
            pl.BlockSpec((1, TQ_SWA, SWA_WIDTH), lambda b, i: (b, i, A_Q_SWA // SWA_WIDTH)),
            pl.BlockSpec((1, SEQ, LANES), lambda b, i: (b, 0, A_K_SWA // LANES)),
            pl.BlockSpec((1, SEQ, LANES), lambda b, i: (b, 0, A_V_SWA // LANES)),
        ],
        out_specs=pl.BlockSpec((1, TQ_SWA, SWA_WIDTH), lambda b, i: (b, i, 0)),
        out_shape=jax.ShapeDtypeStruct((BATCH, SEQ, SWA_WIDTH), BF16),
        compiler_params=pltpu.CompilerParams(
            dimension_semantics=("arbitrary", "arbitrary"),
            vmem_limit_bytes=VMEM_LIMIT_BYTES),
        name="swa_attn",
    )(sinks, proj_a, proj_a, proj_a)


def _post_kernel(x_ref, od_ref, os_ref, wo_ref, gpost1_ref, gt1_ref, gpre2_ref, sc2_ref, sh2_ref,
                 wg_ref, wu_ref, wd_ref, gpost2_ref, gt2_ref, o_ref):
    x = x_ref[0]
    y = jnp.dot(od_ref[0], wo_ref[0:DIFF_WIDTH, :], preferred_element_type=F32)
    y = y + jnp.dot(os_ref[0], wo_ref[DIFF_WIDTH:D_MODEL, :], preferred_element_type=F32)
    x1 = x + gt1_ref[0] * (_rms(y) * gpost1_ref[...])
    h = (_rms(x1) * gpre2_ref[...]) * (1.0 + sc2_ref[0]) + sh2_ref[0]
    hb = h.astype(BF16)
    gate = jnp.dot(hb, wg_ref[...], preferred_element_type=F32)
    up = jnp.dot(hb, wu_ref[...], preferred_element_type=F32)
    act = (gate / (1.0 + jnp.exp(-gate))) * up
    y2 = jnp.dot(act.astype(BF16), wd_ref[...], preferred_element_type=F32)
    o_ref[0] = x1 + gt2_ref[0] * (_rms(y2) * gpost2_ref[...])


def _post_call(layer, x, o_diff, o_swa, mod3, w_out_b, g_post1, g_pre2, w_gate_b, w_up_b, w_down_b, g_post2):
    n_t = SEQ // TM_POST
    row = lambda: pl.BlockSpec((1, D_MODEL), lambda b, i: (0, 0))
    return pl.pallas_call(
        _post_kernel,
        grid=(BATCH, n_t),
        in_specs=[
            pl.BlockSpec((1, TM_POST, D_MODEL), lambda b, i: (b, i, 0)),
            pl.BlockSpec((1, TM_POST, DIFF_WIDTH), lambda b, i: (b, i, 0)),
            pl.BlockSpec((1, TM_POST, SWA_WIDTH), lambda b, i: (b, i, 0)),
            _resident((D_MODEL, D_MODEL)),
            row(),
            _mod_spec(layer, 2),
            row(),
            _mod_spec(layer, 4),
            _mod_spec(layer, 3),
            _resident((D_MODEL, D_FF)),
            _resident((D_MODEL, D_FF)),
            _resident((D_FF, D_MODEL)),
            row(),
            _mod_spec(layer, 5),
        ],
        out_specs=pl.BlockSpec((1, TM_POST, D_MODEL), lambda b, i: (b, i, 0)),
        out_shape=jax.ShapeDtypeStruct((BATCH, SEQ, D_MODEL), F32),
        compiler_params=pltpu.CompilerParams(
            dimension_semantics=("arbitrary", "arbitrary"),
            vmem_limit_bytes=VMEM_LIMIT_BYTES),
        name="post_ffn",
    )(x, o_diff, o_swa, w_out_b, g_post1.reshape(1, D_MODEL), mod3, g_pre2.reshape(1, D_MODEL),
      mod3, mod3, w_gate_b, w_up_b, w_down_b, g_post2.reshape(1, D_MODEL), mod3)


def kernel(x, c, ada_w, ada_b, g_mix_pre, g_mix_post, g_ffn_pre, g_ffn_post, w_in, lambda_q1, lambda_k1,
           lambda_q2, lambda_k2, subln_g, sinks, w_out, w_gate, w_up, w_down):
    cols_a, scale_a, cols_t, scale_t = _in_proj_layout()
    out_perm = _out_proj_row_order()
    tables = _rope_tables()

    mod = _ada_call(c, ada_w, ada_b)
    mod3 = mod.reshape(DEPTH * BATCH * 6, 1, D_MODEL)

    for layer in range(DEPTH):
        w_a = (w_in[layer][:, cols_a] * scale_a[None, :]).astype(BF16)
        w_t = (w_in[layer][:, cols_t] * scale_t[None, :]).T.astype(BF16)
        w_out_b = w_out[layer][out_perm, :].astype(BF16)
        lam_vec = jnp.stack([lambda_q1[layer], lambda_k1[layer], lambda_q2[layer], lambda_k2[layer]])

        proj_a, proj_t = _pre_call(layer, x, mod3, g_mix_pre[layer], w_a, w_t, tables)
        o_diff = _diff_call(layer, proj_a, proj_t, lam_vec, subln_g[layer])
        o_swa = _swa_call(proj_a, sinks[layer])
        x = _post_call(layer, x, o_diff, o_swa, mod3, w_out_b, g_mix_post[layer], g_ffn_pre[layer],
                       w_gate[layer].astype(BF16), w_up[layer].astype(BF16), w_down[layer].astype(BF16),
                       g_ffn_post[layer])
    return x
```

```python
import functools
import math

import numpy as np
import jax
import jax.numpy as jnp
from jax import lax
from jax.experimental import pallas as pl
from jax.experimental.pallas import tpu as pltpu

D_MODEL = 1024
BATCH = 16
SEQ = 2048
DEPTH = 2
HEAD_DIM = 64
HALF = HEAD_DIM // 2
DIFF_HEADS = 4
DIFF_WIDTH = 512
SWA_HEADS = 8
SWA_KV_HEADS = 2
SWA_GROUP = SWA_HEADS // SWA_KV_HEADS
SWA_WIDTH = 512
WINDOW = 128
ROPE_THETA = 10000.0
D_FF = 2816
EPS = 1e-6
NEG = -1e30
LOG2E = math.log2(math.e)

LANES = 128

REF_Q_DIFF, REF_K_DIFF, REF_V_DIFF, REF_Q_SWA, REF_K_SWA, REF_V_SWA = 0, 512, 1024, 1536, 2048, 2176
A_K_DIFF, A_Q_SWA, A_K_SWA, A_V_SWA, A_WIDTH = 0, 512, 1024, 1152, 1280
A_ROPE_GROUPS = 9
T_Q_DIFF, T_V_DIFF, T_WIDTH = 0, 512, 1024

VMEM_LIMIT_BYTES = 56 * 1024 * 1024

TM_PRE = 512
TM_POST = 256
T_DIFF = 256
DENOM_ROWS = 16
TQ_SWA = 512

F32 = jnp.float32
BF16 = jnp.bfloat16
NT_DIMS = (((1,), (1,)), ((), ()))


def _in_proj_layout():
    j = np.arange(LANES)
    second, which, d = j // 64, (j // HALF) % 2, j % HALF
    pair_order = which * HEAD_DIM + second * HALF + d
    score_scale = HEAD_DIM ** -0.5

    cols_a = np.zeros(A_WIDTH, np.int64)
    scale_a = np.ones(A_WIDTH, np.float32)
    for h in range(DIFF_HEADS):
        cols_a[A_K_DIFF + h * LANES + j] = REF_K_DIFF + h * LANES + pair_order
    for g in range(SWA_GROUP):
        cols_a[A_Q_SWA + g * LANES + j] = REF_Q_SWA + (which * SWA_GROUP + g) * HEAD_DIM + second * HALF + d
    scale_a[A_Q_SWA:A_Q_SWA + SWA_WIDTH] = score_scale
    cols_a[A_K_SWA + j] = REF_K_SWA + pair_order
    cols_a[A_V_SWA + j] = REF_V_SWA + j

    cols_t = np.zeros(T_WIDTH, np.int64)
    scale_t = np.ones(T_WIDTH, np.float32)
    for h in range(DIFF_HEADS):
        cols_t[T_Q_DIFF + h * LANES + j] = REF_Q_DIFF + h * LANES + pair_order
    scale_t[T_Q_DIFF:T_Q_DIFF + DIFF_WIDTH] = score_scale
    cols_t[T_V_DIFF:T_V_DIFF + DIFF_WIDTH] = REF_V_DIFF + np.arange(DIFF_WIDTH)
    return cols_a, scale_a, cols_t, scale_t


def _out_proj_row_order():
    perm = np.arange(D_MODEL)
    d = np.arange(HEAD_DIM)
    for g in range(SWA_GROUP):
        for t in range(SWA_KV_HEADS):
            perm[DIFF_WIDTH + g * LANES + t * HEAD_DIM + d] = DIFF_WIDTH + (t * SWA_GROUP + g) * HEAD_DIM + d
    return perm


def _rope_tables():
    pos = jnp.arange(SEQ, dtype=F32)
    inv = ROPE_THETA ** (-jnp.arange(0, HEAD_DIM, 2, dtype=F32) / HEAD_DIM)
    ang = pos[:, None] * inv[None, :]
    cos, sin = jnp.cos(ang), jnp.sin(ang)
    cos4 = jnp.concatenate([cos, cos, cos, cos], axis=-1)
    sin4 = jnp.concatenate([-sin, -sin, sin, sin], axis=-1)
    cos_t = jnp.concatenate([cos, cos], axis=-1).T * LOG2E
    sin_t = jnp.concatenate([sin, sin], axis=-1).T * LOG2E
    return cos4, sin4, cos_t, sin_t


def _rms(x):
    return x * lax.rsqrt(jnp.mean(x * x, axis=-1, keepdims=True) + EPS)


def _ada_kernel(c_ref, w_ref, b_ref, o_ref):
    c = c_ref[...]
    c_act = c / (1.0 + jnp.exp(-c))
    o_ref[0] = jnp.dot(c_act, w_ref[0], preferred_element_type=F32) + b_ref[0]


def _ada_call(c, ada_w, ada_b):
    n_chunk = 6
    return pl.pallas_call(
        _ada_kernel,
        grid=(DEPTH, n_chunk),
        in_specs=[
            pl.BlockSpec((BATCH, D_MODEL), lambda l, k: (0, 0)),
            pl.BlockSpec((1, D_MODEL, D_MODEL), lambda l, k: (l, 0, k)),
            pl.BlockSpec((1, 1, D_MODEL), lambda l, k: (l, 0, k)),
        ],
        out_specs=pl.BlockSpec((1, BATCH, D_MODEL), lambda l, k: (l, 0, k)),
        out_shape=jax.ShapeDtypeStruct((DEPTH, BATCH, 6 * D_MODEL), F32),
        name="ada_mod",
    )(c, ada_w, ada_b.reshape(DEPTH, 1, 6 * D_MODEL))


def _pre_kernel(x_ref, g_ref, sh_ref, sc_ref, wa_ref, wt_ref, cos_ref, sin_ref, cost_ref, sint_ref,
                oa_ref, ot_ref):
    x = x_ref[0]
    h = (_rms(x) * g_ref[...]) * (1.0 + sc_ref[0]) + sh_ref[0]
    hb = h.astype(BF16)

    proj = jnp.dot(hb, wa_ref[...], preferred_element_type=F32)
    cos = cos_ref[...]
    sin = sin_ref[...]
    for grp in range(A_WIDTH // LANES):
        cols = slice(grp * LANES, (grp + 1) * LANES)
        xg = proj[:, cols]
        if grp < A_ROPE_GROUPS:
            xg = xg * cos + pltpu.roll(xg, 64, 1) * sin
        oa_ref[0, :, cols] = xg.astype(BF16)

    proj_t = lax.dot_general(wt_ref[...], hb, NT_DIMS, preferred_element_type=F32)
    cos_t = cost_ref[...]
    sin_t = sint_ref[...]
    n_blk = TM_PRE // T_DIFF
    for hd in range(DIFF_HEADS):
        lo = proj_t[T_Q_DIFF + hd * LANES:T_Q_DIFF + hd * LANES + 64]
        hi = proj_t[T_Q_DIFF + hd * LANES + 64:T_Q_DIFF + (hd + 1) * LANES]
        out_lo = (lo * cos_t - hi * sin_t).astype(BF16)
        out_hi = (hi * cos_t + lo * sin_t).astype(BF16)
        for blk in range(n_blk):
            tok = slice(blk * T_DIFF, (blk + 1) * T_DIFF)
            ot_ref[0, blk, T_Q_DIFF + hd * LANES:T_Q_DIFF + hd * LANES + 64, :] = out_lo[:, tok]
            ot_ref[0, blk, T_Q_DIFF + hd * LANES + 64:T_Q_DIFF + (hd + 1) * LANES, :] = out_hi[:, tok]
    for blk in range(n_blk):
        tok = slice(blk * T_DIFF, (blk + 1) * T_DIFF)
        ot_ref[0, blk, T_V_DIFF:T_WIDTH, :] = proj_t[T_V_DIFF:T_WIDTH, tok].astype(BF16)


def _mod_spec(layer, which):
    def index(b, i):
        return ((layer * BATCH + b) * 6 + which, 0, 0)
    return pl.BlockSpec((1, 1, D_MODEL), index)


def _resident(shape):
    return pl.BlockSpec(shape, lambda b, i: (0,) * len(shape), pipeline_mode=pl.Buffered(1))


def _pre_call(layer, x, mod3, g_pre, w_a, w_t, tables):
    cos4, sin4, cos_t, sin_t = tables
    n_t = SEQ // TM_PRE
    n_blk = TM_PRE // T_DIFF
    return pl.pallas_call(
        _pre_kernel,
        grid=(BATCH, n_t),
        in_specs=[
            pl.BlockSpec((1, TM_PRE, D_MODEL), lambda b, i: (b, i, 0)),
            pl.BlockSpec((1, D_MODEL), lambda b, i: (0, 0)),
            _mod_spec(layer, 0),
            _mod_spec(layer, 1),
            _resident((D_MODEL, A_WIDTH)),
            _resident((T_WIDTH, D_MODEL)),
            pl.BlockSpec((TM_PRE, LANES), lambda b, i: (i, 0)),
            pl.BlockSpec((TM_PRE, LANES), lambda b, i: (i, 0)),
            pl.BlockSpec((HEAD_DIM, TM_PRE), lambda b, i: (0, i)),
            pl.BlockSpec((HEAD_DIM, TM_PRE), lambda b, i: (0, i)),
        ],
        out_specs=[
            pl.BlockSpec((1, TM_PRE, A_WIDTH), lambda b, i: (b, i, 0)),
            pl.BlockSpec((1, n_blk, T_WIDTH, T_DIFF), lambda b, i: (b, i, 0, 0)),
        ],
        out_shape=[
            jax.ShapeDtypeStruct((BATCH, SEQ, A_WIDTH), BF16),
            jax.ShapeDtypeStruct((BATCH, SEQ // T_DIFF, T_WIDTH, T_DIFF), BF16),
        ],
        compiler_params=pltpu.CompilerParams(
            dimension_semantics=("arbitrary", "arbitrary"),
            vmem_limit_bytes=VMEM_LIMIT_BYTES),
        name="pre_mixer",
    )(x, g_pre.reshape(1, D_MODEL), mod3, mod3, w_a, w_t, cos4, sin4, cos_t, sin_t)


def _diff_kernel(lam_ref, g_ref, qt_ref, k_ref, vt_ref, o_ref,
                 qbd_ref, m_ref, acc_ref, sa_ref, sb_ref, maxa_ref, maxb_ref, *, lam_init):
    i = pl.program_id(1)
    t = T_DIFF
    feat = lax.broadcasted_iota(jnp.int32, (LANES, 1), 0)
    first_map = ((feat // HALF) % 2) == 0

    for h in range(DIFF_HEADS):
        qh = qt_ref[0, 0, h * LANES:(h + 1) * LANES, :]
        zero = jnp.zeros_like(qh)
        qbd_ref[h, :, 0:t] = jnp.where(first_map, qh, zero)
        qbd_ref[h, :, t:2 * t] = jnp.where(first_map, zero, qh)
        m_ref[h] = jnp.full((1, 2 * t), NEG, F32)
        acc_ref[h] = jnp.zeros((LANES + DENOM_ROWS, 2 * t), F32)
    ones_rows = jnp.ones((DENOM_ROWS, t), BF16)

    def scores(j, s_ref, max_ref):
        start = pl.multiple_of(j * t, t)
        for h in range(DIFF_HEADS):
            kh = k_ref[0, pl.ds(start, t), h * LANES:(h + 1) * LANES]
            s = jnp.dot(kh, qbd_ref[h], preferred_element_type=F32)
            s_ref[h] = s
            max_ref[h] = jnp.max(s, axis=0, keepdims=True)

    def accumulate(j, s_ref, max_ref, diagonal):
        for h in range(DIFF_HEADS):
            vth = vt_ref[0, j, h * LANES:(h + 1) * LANES, :]
            s = s_ref[h]
            if diagonal:
                key = lax.broadcasted_iota(jnp.int32, (t, 2 * t), 0)
                qry = lax.broadcasted_iota(jnp.int32, (t, 2 * t), 1) & (t - 1)
                s = jnp.where(key <= qry, s, NEG)
                m_blk = jnp.max(s, axis=0, keepdims=True)
            else:
                m_blk = max_ref[h]
            m_old = m_ref[h]
            m_new = jnp.maximum(m_old, m_blk)
            p = jnp.exp2(s - m_new)
            alpha = jnp.exp2(m_old - m_new)
            v_aug = jnp.concatenate([vth, ones_rows], axis=0)
            acc_ref[h] = alpha * acc_ref[h] + jnp.dot(v_aug, p.astype(BF16), preferred_element_type=F32)
            m_ref[h] = m_new

    scores(0, sa_ref, maxa_ref)

    def pair(step, carry):
        j = 2 * step
        scores(j + 1, sb_ref, maxb_ref)
        accumulate(j, sa_ref, maxa_ref, False)
        scores(j + 2, sa_ref, maxa_ref)
        accumulate(j + 1, sb_ref, maxb_ref, False)
        return carry

    lax.fori_loop(0, i // 2, pair, 0)

    @pl.when(i % 2 == 1)
    def _():
        scores(i, sb_ref, maxb_ref)
        accumulate(i - 1, sa_ref, maxa_ref, False)
        accumulate(i, sb_ref, maxb_ref, True)

    @pl.when(i % 2 == 0)
    def _():
        accumulate(i, sa_ref, maxa_ref, True)

    lam_vec = lam_ref[...]
    dot1 = jnp.sum(lam_vec[0:1] * lam_vec[1:2], axis=-1, keepdims=True)
    dot2 = jnp.sum(lam_vec[2:3] * lam_vec[3:4], axis=-1, keepdims=True)
    lam = jnp.exp(dot1) - jnp.exp(dot2) + lam_init
    g = g_ref[...]
    for h in range(DIFF_HEADS):
        denom = acc_ref[h, LANES:LANES + 1, :]
        o_all = acc_ref[h, 0:LANES, :] * (1.0 / denom)
        o = o_all[:, 0:t] - lam * o_all[:, t:2 * t]
        o = o * lax.rsqrt(jnp.mean(o * o, axis=0, keepdims=True) + EPS)
        o = (o * g) * (1.0 - lam_init)
        o_ref[0, :, h * LANES:(h + 1) * LANES] = o.T.astype(BF16)


def _diff_call(layer, proj_a, proj_t, lam_vec, subln_g):
    lam_init = 0.8 - 0.6 * math.exp(-0.3 * layer)
    n_q = SEQ // T_DIFF
    kernel = functools.partial(_diff_kernel, lam_init=lam_init)
    return pl.pallas_call(
        kernel,
        grid=(BATCH, n_q),
        in_specs=[
            pl.BlockSpec((4, HEAD_DIM), lambda b, i: (0, 0)),
            pl.BlockSpec((LANES, 1), lambda b, i: (0, 0)),
            pl.BlockSpec((1, 1, DIFF_WIDTH, T_DIFF), lambda b, i: (b, i, T_Q_DIFF // DIFF_WIDTH, 0)),
            pl.BlockSpec((1, SEQ, DIFF_WIDTH), lambda b, i: (b, 0, A_K_DIFF // DIFF_WIDTH)),
            pl.BlockSpec((1, n_q, DIFF_WIDTH, T_DIFF), lambda b, i: (b, 0, T_V_DIFF // DIFF_WIDTH, 0)),
        ],
        out_specs=pl.BlockSpec((1, T_DIFF, DIFF_WIDTH), lambda b, i: (b, i, 0)),
        out_shape=jax.ShapeDtypeStruct((BATCH, SEQ, DIFF_WIDTH), BF16),
        scratch_shapes=[
            pltpu.VMEM((DIFF_HEADS, LANES, 2 * T_DIFF), BF16),
            pltpu.VMEM((DIFF_HEADS, 1, 2 * T_DIFF), F32),
            pltpu.VMEM((DIFF_HEADS, LANES + DENOM_ROWS, 2 * T_DIFF), F32),
            pltpu.VMEM((DIFF_HEADS, T_DIFF, 2 * T_DIFF), F32),
            pltpu.VMEM((DIFF_HEADS, T_DIFF, 2 * T_DIFF), F32),
            pltpu.VMEM((DIFF_HEADS, 1, 2 * T_DIFF), F32),
            pltpu.VMEM((DIFF_HEADS, 1, 2 * T_DIFF), F32),
        ],
        compiler_params=pltpu.CompilerParams(
            dimension_semantics=("arbitrary", "arbitrary"),
            vmem_limit_bytes=VMEM_LIMIT_BYTES),
        name="diff_attn",
    )(lam_vec, subln_g.reshape(LANES, 1), proj_t, proj_a, proj_t)


def _swa_kernel(sink_ref, q_ref, k_ref, v_ref, o_ref):
    i = pl.program_id(1)
    n_sub = TQ_SWA // WINDOW
    band = 2 * WINDOW
    lane = lax.broadcasted_iota(jnp.int32, (1, LANES), 1)
    kv0_lanes = ((lane // HALF) % 2) == 0
    r = lax.broadcasted_iota(jnp.int32, (WINDOW, band), 0)
    c = lax.broadcasted_iota(jnp.int32, (WINDOW, band), 1)
    for n in range(n_sub):
        blk = i * n_sub + n
        kstart = pl.multiple_of(jnp.maximum(blk - 1, 0) * WINDOW, WINDOW)
        kb = k_ref[0, pl.ds(kstart, band), :]
        vb = v_ref[0, pl.ds(kstart, band), :]
        dist = (blk * WINDOW - kstart) + r - c
        valid = (dist >= 0) & (dist < WINDOW)
        for g in range(SWA_GROUP):
            qg = q_ref[0, n * WINDOW:(n + 1) * WINDOW, g * LANES:(g + 1) * LANES]
            zero = jnp.zeros_like(qg)
            outs = []
            for kvh in range(SWA_KV_HEADS):
                qt = jnp.where(kv0_lanes, qg, zero) if kvh == 0 else jnp.where(kv0_lanes, zero, qg)
                sink = sink_ref[kvh * SWA_GROUP + g]
                s = lax.dot_general(qt, kb, NT_DIMS, preferred_element_type=F32)
                s = jnp.where(valid, s, NEG)
                m = jnp.maximum(jnp.max(s, axis=-1, keepdims=True), sink)
                p = jnp.exp(s - m)
                denom = jnp.sum(p, axis=-1, keepdims=True) + jnp.exp(sink - m)
                outs.append(jnp.dot(p.astype(BF16), vb, preferred_element_type=F32) / denom)
            og = jnp.where(lane < HEAD_DIM, outs[0], outs[1])
            o_ref[0, n * WINDOW:(n + 1) * WINDOW, g * LANES:(g + 1) * LANES] = og.astype(BF16)


def _swa_call(proj_a, sinks):
    n_q = SEQ // TQ_SWA
    return pl.pallas_call(
        _swa_kernel,
        grid=(BATCH, n_q),
        in_specs=[
            pl.BlockSpec(memory_space=pltpu.SMEM),
            pl.BlockSpec((1, TQ_SWA, SWA_WIDTH), lambda b, i: (b, i, A_Q_SWA // SWA_WIDTH)),
            pl.BlockSpec((1, SEQ, LANES), lambda b, i: (b, 0, A_K_SWA // LANES)),
            pl.BlockSpec((1, SEQ, LANES), lambda b, i: (b, 0, A_V_SWA // LANES)),
        ],
        out_specs=pl.BlockSpec((1, TQ_SWA, SWA_WIDTH), lambda b, i: (b, i, 0)),
        out_shape=jax.ShapeDtypeStruct((BATCH, SEQ, SWA_WIDTH), BF16),
        compiler_params=pltpu.CompilerParams(
            dimension_semantics=("arbitrary", "arbitrary"),
            vmem_limit_bytes=VMEM_LIMIT_BYTES),
        name="swa_attn",
    )(sinks, proj_a, proj_a, proj_a)


def _post_kernel(x_ref, od_ref, os_ref, wo_ref, gpost1_ref, gt1_ref, gpre2_ref, sc2_ref, sh2_ref,
                 wg_ref, wu_ref, wd_ref, gpost2_ref, gt2_ref, o_ref):
    x = x_ref[0]
    y = jnp.dot(od_ref[0], wo_ref[0:DIFF_WIDTH, :], preferred_element_type=F32)
    y = y + jnp.dot(os_ref[0], wo_ref[DIFF_WIDTH:D_MODEL, :], preferred_element_type=F32)
    x1 = x + gt1_ref[0] * (_rms(y) * gpost1_ref[...])
    h = (_rms(x1) * gpre2_ref[...]) * (1.0 + sc2_ref[0]) + sh2_ref[0]
    hb = h.astype(BF16)
    gate = jnp.dot(hb, wg_ref[...], preferred_element_type=F32)
    up = jnp.dot(hb, wu_ref[...], preferred_element_type=F32)
    act = (gate / (1.0 + jnp.exp(-gate))) * up
    y2 = jnp.dot(act.astype(BF16), wd_ref[...], preferred_element_type=F32)
    o_ref[0] = x1 + gt2_ref[0] * (_rms(y2) * gpost2_ref[...])


def _post_call(layer, x, o_diff, o_swa, mod3, w_out_b, g_post1, g_pre2, w_gate_b, w_up_b, w_down_b, g_post2):
    n_t = SEQ // TM_POST
    row = lambda: pl.BlockSpec((1, D_MODEL), lambda b, i: (0, 0))
    return pl.pallas_call(
        _post_kernel,
        grid=(BATCH, n_t),
        in_specs=[
            pl.BlockSpec((1, TM_POST, D_MODEL), lambda b, i: (b, i, 0)),
            pl.BlockSpec((1, TM_POST, DIFF_WIDTH), lambda b, i: (b, i, 0)),
            pl.BlockSpec((1, TM_POST, SWA_WIDTH), lambda b, i: (b, i, 0)),
            _resident((D_MODEL, D_MODEL)),
            row(),
            _mod_spec(layer, 2),
            row(),
            _mod_spec(layer, 4),
            _mod_spec(layer, 3),
            _resident((D_MODEL, D_FF)),
            _resident((D_MODEL, D_FF)),
            _resident((D_FF, D_MODEL)),
            row(),
            _mod_spec(layer, 5),
        ],
        out_specs=pl.BlockSpec((1, TM_POST, D_MODEL), lambda b, i: (b, i, 0)),
        out_shape=jax.ShapeDtypeStruct((BATCH, SEQ, D_MODEL), F32),
        compiler_params=pltpu.CompilerParams(
            dimension_semantics=("arbitrary", "arbitrary"),
            vmem_limit_bytes=VMEM_LIMIT_BYTES),
        name="post_ffn",
    )(x, o_diff, o_swa, w_out_b, g_post1.reshape(1, D_MODEL), mod3, g_pre2.reshape(1, D_MODEL),
      mod3, mod3, w_gate_b, w_up_b, w_down_b, g_post2.reshape(1, D_MODEL), mod3)


def kernel(x, c, ada_w, ada_b, g_mix_pre, g_mix_post, g_ffn_pre, g_ffn_post, w_in, lambda_q1, lambda_k1,
           lambda_q2, lambda_k2, subln_g, sinks, w_out, w_gate, w_up, w_down):
    cols_a, scale_a, cols_t, scale_t = _in_proj_layout()
    out_perm = _out_proj_row_order()
    tables = _rope_tables()

    mod = _ada_call(c, ada_w, ada_b)
    mod3 = mod.reshape(DEPTH * BATCH * 6, 1, D_MODEL)

    for layer in range(DEPTH):
        w_a = (w_in[layer][:, cols_a] * scale_a[None, :]).astype(BF16)
        w_t = (w_in[layer][:, cols_t] * scale_t[None, :]).T.astype(BF16)
        w_out_b = w_out[layer][out_perm, :].astype(BF16)
        lam_vec = jnp.stack([lambda_q1[layer], lambda_k1[layer], lambda_q2[layer], lambda_k2[layer]])

        proj_a, proj_t = _pre_call(layer, x, mod3, g_mix_pre[layer], w_a, w_t, tables)
        o_diff = _diff_call(layer, proj_a, proj_t, lam_vec, subln_g[layer])
        o_swa = _swa_call(proj_a, sinks[layer])
        x = _post_call(layer, x, o_diff, o_swa, mod3, w_out_b, g_mix_post[layer], g_ffn_pre[layer],
                       w_gate[layer].astype(BF16), w_up[layer].astype(BF16), w_down[layer].astype(BF16),
                       g_ffn_post[layer])
    return x
```

```python
import functools
import math

import numpy as np
import jax
import jax.numpy as jnp
from jax import lax
from jax.experimental import pallas as pl
from jax.experimental.pallas import tpu as pltpu

D_MODEL = 1024
BATCH = 16
SEQ = 2048
DEPTH = 2
HEAD_DIM = 64
HALF = HEAD_DIM // 2
DIFF_HEADS = 4
DIFF_WIDTH = 512
SWA_HEADS = 8
SWA_KV_HEADS = 2
SWA_GROUP = SWA_HEADS // SWA_KV_HEADS
SWA_WIDTH = 512
WINDOW = 128
ROPE_THETA = 10000.0
D_FF = 2816
EPS = 1e-6
NEG = -1e30
LOG2E = math.log2(math.e)

LANES = 128

REF_Q_DIFF, REF_K_DIFF, REF_V_DIFF, REF_Q_SWA, REF_K_SWA, REF_V_SWA = 0, 512, 1024, 1536, 2048, 2176
A_K_DIFF, A_Q_SWA, A_K_SWA, A_V_SWA, A_WIDTH = 0, 512, 1024, 1152, 1280
A_ROPE_GROUPS = 9
T_Q_DIFF, T_V_DIFF, T_WIDTH = 0, 512, 1024

VMEM_LIMIT_BYTES = 56 * 1024 * 1024

TM_PRE = 512
TM_POST = 512
SUB_POST = 256
T_DIFF = 256
DENOM_ROWS = 16
TQ_SWA = 512

F32 = jnp.float32
BF16 = jnp.bfloat16
NT_DIMS = (((1,), (1,)), ((), ()))


def _in_proj_layout():
    j = np.arange(LANES)
    second, which, d = j // 64, (j // HALF) % 2, j % HALF
    pair_order = which * HEAD_DIM + second * HALF + d
    score_scale = HEAD_DIM ** -0.5

    cols_a = np.zeros(A_WIDTH, np.int64)
    scale_a = np.ones(A_WIDTH, np.float32)
    for h in range(DIFF_HEADS):
        cols_a[A_K_DIFF + h * LANES + j] = REF_K_DIFF + h * LANES + pair_order
    for g in range(SWA_GROUP):
        cols_a[A_Q_SWA + g * LANES + j] = REF_Q_SWA + (which * SWA_GROUP + g) * HEAD_DIM + second * HALF + d
    scale_a[A_Q_SWA:A_Q_SWA + SWA_WIDTH] = score_scale
    cols_a[A_K_SWA + j] = REF_K_SWA + pair_order
    cols_a[A_V_SWA + j] = REF_V_SWA + j

    cols_t = np.zeros(T_WIDTH, np.int64)
    scale_t = np.ones(T_WIDTH, np.float32)
    for h in range(DIFF_HEADS):
        cols_t[T_Q_DIFF + h * LANES + j] = REF_Q_DIFF + h * LANES + pair_order
    scale_t[T_Q_DIFF:T_Q_DIFF + DIFF_WIDTH] = score_scale
    cols_t[T_V_DIFF:T_V_DIFF + DIFF_WIDTH] = REF_V_DIFF + np.arange(DIFF_WIDTH)
    return cols_a, scale_a, cols_t, scale_t


def _out_proj_row_order():
    perm = np.arange(D_MODEL)
    d = np.arange(HEAD_DIM)
    for g in range(SWA_GROUP):
        for t in range(SWA_KV_HEADS):
            perm[DIFF_WIDTH + g * LANES + t * HEAD_DIM + d] = DIFF_WIDTH + (t * SWA_GROUP + g) * HEAD_DIM + d
    return perm


def _rope_tables():
    pos = jnp.arange(SEQ, dtype=F32)
    inv = ROPE_THETA ** (-jnp.arange(0, HEAD_DIM, 2, dtype=F32) / HEAD_DIM)
    ang = pos[:, None] * inv[None, :]
    cos, sin = jnp.cos(ang), jnp.sin(ang)
    cos4 = jnp.concatenate([cos, cos, cos, cos], axis=-1)
    sin4 = jnp.concatenate([-sin, -sin, sin, sin], axis=-1)
    cos_t = jnp.concatenate([cos, cos], axis=-1).T * LOG2E
    sin_t = jnp.concatenate([sin, sin], axis=-1).T * LOG2E
    return cos4, sin4, cos_t, sin_t


def _rms(x):
    return x * lax.rsqrt(jnp.mean(x * x, axis=-1, keepdims=True) + EPS)


def _ada_kernel(c_ref, w_ref, b_ref, o_ref):
    c = c_ref[...]
    c_act = c / (1.0 + jnp.exp(-c))
    o_ref[0] = jnp.dot(c_act, w_ref[0], preferred_element_type=F32) + b_ref[0]


def _ada_call(c, ada_w, ada_b):
    n_chunk = 6
    return pl.pallas_call(
        _ada_kernel,
        grid=(DEPTH, n_chunk),
        in_specs=[
            pl.BlockSpec((BATCH, D_MODEL), lambda l, k: (0, 0)),
            pl.BlockSpec((1, D_MODEL, D_MODEL), lambda l, k: (l, 0, k)),
            pl.BlockSpec((1, 1, D_MODEL), lambda l, k: (l, 0, k)),
        ],
        out_specs=pl.BlockSpec((1, BATCH, D_MODEL), lambda l, k: (l, 0, k)),
        out_shape=jax.ShapeDtypeStruct((DEPTH, BATCH, 6 * D_MODEL), F32),
        name="ada_mod",
    )(c, ada_w, ada_b.reshape(DEPTH, 1, 6 * D_MODEL))


def _pre_kernel(x_ref, g_ref, sh_ref, sc_ref, wa_ref, wt_ref, cos_ref, sin_ref, cost_ref, sint_ref,
                oa_ref, ot_ref):
    x = x_ref[0]
    h = (_rms(x) * g_ref[...]) * (1.0 + sc_ref[0]) + sh_ref[0]
    hb = h.astype(BF16)

    proj = jnp.dot(hb, wa_ref[...], preferred_element_type=F32)
    cos = cos_ref[...]
    sin = sin_ref[...]
    for grp in range(A_WIDTH // LANES):
        cols = slice(grp * LANES, (grp + 1) * LANES)
        xg = proj[:, cols]
        if grp < A_ROPE_GROUPS:
            xg = xg * cos + pltpu.roll(xg, 64, 1) * sin
        oa_ref[0, :, cols] = xg.astype(BF16)

    proj_t = lax.dot_general(wt_ref[...], hb, NT_DIMS, preferred_element_type=F32)
    cos_t = cost_ref[...]
    sin_t = sint_ref[...]
    n_blk = TM_PRE // T_DIFF
    for hd in range(DIFF_HEADS):
        lo = proj_t[T_Q_DIFF + hd * LANES:T_Q_DIFF + hd * LANES + 64]
        hi = proj_t[T_Q_DIFF + hd * LANES + 64:T_Q_DIFF + (hd + 1) * LANES]
        out_lo = (lo * cos_t - hi * sin_t).astype(BF16)
        out_hi = (hi * cos_t + lo * sin_t).astype(BF16)
        for blk in range(n_blk):
            tok = slice(blk * T_DIFF, (blk + 1) * T_DIFF)
            ot_ref[0, blk, T_Q_DIFF + hd * LANES:T_Q_DIFF + hd * LANES + 64, :] = out_lo[:, tok]
            ot_ref[0, blk, T_Q_DIFF + hd * LANES + 64:T_Q_DIFF + (hd + 1) * LANES, :] = out_hi[:, tok]
    for blk in range(n_blk):
        tok = slice(blk * T_DIFF, (blk + 1) * T_DIFF)
        ot_ref[0, blk, T_V_DIFF:T_WIDTH, :] = proj_t[T_V_DIFF:T_WIDTH, tok].astype(BF16)


def _mod_spec(layer, which):
    def index(b, i):
        return ((layer * BATCH + b) * 6 + which, 0, 0)
    return pl.BlockSpec((1, 1, D_MODEL), index)


def _resident(shape):
    return pl.BlockSpec(shape, lambda b, i: (0,) * len(shape), pipeline_mode=pl.Buffered(1))


def _pre_call(layer, x, mod3, g_pre, w_a, w_t, tables):
    cos4, sin4, cos_t, sin_t = tables
    n_t = SEQ // TM_PRE
    n_blk = TM_PRE // T_DIFF
    return pl.pallas_call(
        _pre_kernel,
        grid=(BATCH, n_t),
        in_specs=[
            pl.BlockSpec((1, TM_PRE, D_MODEL), lambda b, i: (b, i, 0)),
            pl.BlockSpec((1, D_MODEL), lambda b, i: (0, 0)),
            _mod_spec(layer, 0),
            _mod_spec(layer, 1),
            _resident((D_MODEL, A_WIDTH)),
            _resident((T_WIDTH, D_MODEL)),
            pl.BlockSpec((TM_PRE, LANES), lambda b, i: (i, 0)),
            pl.BlockSpec((TM_PRE, LANES), lambda b, i: (i, 0)),
            pl.BlockSpec((HEAD_DIM, TM_PRE), lambda b, i: (0, i)),
            pl.BlockSpec((HEAD_DIM, TM_PRE), lambda b, i: (0, i)),
        ],
        out_specs=[
            pl.BlockSpec((1, TM_PRE, A_WIDTH), lambda b, i: (b, i, 0)),
            pl.BlockSpec((1, n_blk, T_WIDTH, T_DIFF), lambda b, i: (b, i, 0, 0)),
        ],
        out_shape=[
            jax.ShapeDtypeStruct((BATCH, SEQ, A_WIDTH), BF16),
            jax.ShapeDtypeStruct((BATCH, SEQ // T_DIFF, T_WIDTH, T_DIFF), BF16),
        ],
        compiler_params=pltpu.CompilerParams(
            dimension_semantics=("arbitrary", "arbitrary"),
            vmem_limit_bytes=VMEM_LIMIT_BYTES),
        name="pre_mixer",
    )(x, g_pre.reshape(1, D_MODEL), mod3, mod3, w_a, w_t, cos4, sin4, cos_t, sin_t)


def _diff_kernel(lam_ref, g_ref, qt_ref, k_ref, vt_ref, o_ref,
                 qbd_ref, m_ref, acc_ref, s0_ref, s1_ref, max0_ref, max1_ref, *, lam_init):
    pair_idx = pl.program_id(1)
    t = T_DIFF
    feat = lax.broadcasted_iota(jnp.int32, (LANES, 1), 0)
    first_map = ((feat // HALF) % 2) == 0

    chains = [(sub, h) for sub in range(2) for h in range(DIFF_HEADS)]
    sub_chains = lambda sub: [c for c, (sb, _) in enumerate(chains) if sb == sub]
    all_chains = list(range(len(chains)))

    for c, (sub, h) in enumerate(chains):
        qh = qt_ref[0, sub, h * LANES:(h + 1) * LANES, :]
        zero = jnp.zeros_like(qh)
        qbd_ref[c, :, 0:t] = jnp.where(first_map, qh, zero)
        qbd_ref[c, :, t:2 * t] = jnp.where(first_map, zero, qh)
        m_ref[c] = jnp.full((1, 2 * t), NEG, F32)
        acc_ref[c] = jnp.zeros((LANES + DENOM_ROWS, 2 * t), F32)
    ones_rows = jnp.ones((DENOM_ROWS, t), BF16)

    def scores(j, c, s_ref, max_ref):
        h = chains[c][1]
        start = pl.multiple_of(j * t, t)
        kh = k_ref[0, pl.ds(start, t), h * LANES:(h + 1) * LANES]
        s = jnp.dot(kh, qbd_ref[c], preferred_element_type=F32)
        s_ref[c] = s
        max_ref[c] = jnp.max(s, axis=0, keepdims=True)

    def accumulate(j, c, s_ref, max_ref, diagonal):
        h = chains[c][1]
        vth = vt_ref[0, j, h * LANES:(h + 1) * LANES, :]
        s = s_ref[c]
        if diagonal:
            key = lax.broadcasted_iota(jnp.int32, (t, 2 * t), 0)
            qry = lax.broadcasted_iota(jnp.int32, (t, 2 * t), 1) & (t - 1)
            s = jnp.where(key <= qry, s, NEG)
            m_blk = jnp.max(s, axis=0, keepdims=True)
        else:
            m_blk = max_ref[c]
        m_old = m_ref[c]
        m_new = jnp.maximum(m_old, m_blk)
        p = jnp.exp2(s - m_new)
        alpha = jnp.exp2(m_old - m_new)
        v_aug = jnp.concatenate([vth, ones_rows], axis=0)
        acc_ref[c] = alpha * acc_ref[c] + jnp.dot(v_aug, p.astype(BF16), preferred_element_type=F32)
        m_ref[c] = m_new

    for c in all_chains:
        scores(0, c, s0_ref, max0_ref)

    def two_blocks(step, carry):
        j = 2 * step
        for c in all_chains:
            scores(j + 1, c, s1_ref, max1_ref)
            accumulate(j, c, s0_ref, max0_ref, False)
        for c in all_chains:
            scores(j + 2, c, s0_ref, max0_ref)
            accumulate(j + 1, c, s1_ref, max1_ref, False)
        return carry

    lax.fori_loop(0, pair_idx, two_blocks, 0)

    diag0 = 2 * pair_idx
    for c0, c1 in zip(sub_chains(0), sub_chains(1)):
        scores(diag0 + 1, c1, s1_ref, max1_ref)
        accumulate(diag0, c0, s0_ref, max0_ref, True)
        accumulate(diag0, c1, s0_ref, max0_ref, False)
    for c1 in sub_chains(1):
        accumulate(diag0 + 1, c1, s1_ref, max1_ref, True)

    lam_vec = lam_ref[...]
    dot1 = jnp.sum(lam_vec[0:1] * lam_vec[1:2], axis=-1, keepdims=True)
    dot2 = jnp.sum(lam_vec[2:3] * lam_vec[3:4], axis=-1, keepdims=True)
    lam = jnp.exp(dot1) - jnp.exp(dot2) + lam_init
    g = g_ref[...]
    for c, (sub, h) in enumerate(chains):
        denom = acc_ref[c, LANES:LANES + 1, :]
        o_all = acc_ref[c, 0:LANES, :] * (1.0 / denom)
        o = o_all[:, 0:t] - lam * o_all[:, t:2 * t]
        o = o * lax.rsqrt(jnp.mean(o * o, axis=0, keepdims=True) + EPS)
        o = (o * g) * (1.0 - lam_init)
        o_ref[0, sub * t:(sub + 1) * t, h * LANES:(h + 1) * LANES] = o.T.astype(BF16)


def _diff_call(layer, proj_a, proj_t, lam_vec, subln_g):
    lam_init = 0.8 - 0.6 * math.exp(-0.3 * layer)
    n_blocks = SEQ // T_DIFF
    n_chains = 2 * DIFF_HEADS
    kernel = functools.partial(_diff_kernel, lam_init=lam_init)
    return pl.pallas_call(
        kernel,
        grid=(BATCH, n_blocks // 2),
        in_specs=[
            pl.BlockSpec((4, HEAD_DIM), lambda b, i: (0, 0)),
            pl.BlockSpec((LANES, 1), lambda b, i: (0, 0)),
            pl.BlockSpec((1, 2, DIFF_WIDTH, T_DIFF), lambda b, i: (b, i, T_Q_DIFF // DIFF_WIDTH, 0)),
            pl.BlockSpec((1, SEQ, DIFF_WIDTH), lambda b, i: (b, 0, A_K_DIFF // DIFF_WIDTH)),
            pl.BlockSpec((1, n_blocks, DIFF_WIDTH, T_DIFF), lambda b, i: (b, 0, T_V_DIFF // DIFF_WIDTH, 0)),
        ],
        out_specs=pl.BlockSpec((1, 2 * T_DIFF, DIFF_WIDTH), lambda b, i: (b, i, 0)),
        out_shape=jax.ShapeDtypeStruct((BATCH, SEQ, DIFF_WIDTH), BF16),
        scratch_shapes=[
            pltpu.VMEM((n_chains, LANES, 2 * T_DIFF), BF16),
            pltpu.VMEM((n_chains, 1, 2 * T_DIFF), F32),
            pltpu.VMEM((n_chains, LANES + DENOM_ROWS, 2 * T_DIFF), F32),
            pltpu.VMEM((n_chains, T_DIFF, 2 * T_DIFF), F32),
            pltpu.VMEM((n_chains, T_DIFF, 2 * T_DIFF), F32),
            pltpu.VMEM((n_chains, 1, 2 * T_DIFF), F32),
            pltpu.VMEM((n_chains, 1, 2 * T_DIFF), F32),
        ],
        compiler_params=pltpu.CompilerParams(
            dimension_semantics=("arbitrary", "arbitrary"),
            vmem_limit_bytes=VMEM_LIMIT_BYTES),
        name="diff_attn",
    )(lam_vec, subln_g.reshape(LANES, 1), proj_t, proj_a, proj_t)


def _swa_kernel(sink_ref, q_ref, k_ref, v_ref, o_ref):
    i = pl.program_id(1)
    n_sub = TQ_SWA // WINDOW
    band = 2 * WINDOW
    lane = lax.broadcasted_iota(jnp.int32, (1, LANES), 1)
    kv0_lanes = ((lane // HALF) % 2) == 0
    r = lax.broadcasted_iota(jnp.int32, (WINDOW, band), 0)
    c = lax.broadcasted_iota(jnp.int32, (WINDOW, band), 1)
    for n in range(n_sub):
        blk = i * n_sub + n
        kstart = pl.multiple_of(jnp.maximum(blk - 1, 0) * WINDOW, WINDOW)
        kb = k_ref[0, pl.ds(kstart, band), :]
        vb = v_ref[0, pl.ds(kstart, band), :]
        dist = (blk * WINDOW - kstart) + r - c
        valid = (dist >= 0) & (dist < WINDOW)
        for g in range(SWA_GROUP):
            qg = q_ref[0, n * WINDOW:(n + 1) * WINDOW, g * LANES:(g + 1) * LANES]
            zero = jnp.zeros_like(qg)
            outs = []
            for kvh in range(SWA_KV_HEADS):
                qt = jnp.where(kv0_lanes, qg, zero) if kvh == 0 else jnp.where(kv0_lanes, zero, qg)
                sink = sink_ref[kvh * SWA_GROUP + g]
                s = lax.dot_general(qt, kb, NT_DIMS, preferred_element_type=F32)
                s = jnp.where(valid, s, NEG)
                m = jnp.maximum(jnp.max(s, axis=-1, keepdims=True), sink)
                p = jnp.exp(s - m)
                denom = jnp.sum(p, axis=-1, keepdims=True) + jnp.exp(sink - m)
                outs.append(jnp.dot(p.astype(BF16), vb, preferred_element_type=F32) / denom)
            og = jnp.where(lane < HEAD_DIM, outs[0], outs[1])
            o_ref[0, n * WINDOW:(n + 1) * WINDOW, g * LANES:(g + 1) * LANES] = og.astype(BF16)


def _swa_call(proj_a, sinks):
    n_q = SEQ // TQ_SWA
    return pl.pallas_call(
        _swa_kernel,
        grid=(BATCH, n_q),
        in_specs=[
            pl.BlockSpec(memory_space=pltpu.SMEM),
            pl.BlockSpec((1, TQ_SWA, SWA_WIDTH), lambda b, i: (b, i, A_Q_SWA // SWA_WIDTH)),
            pl.BlockSpec((1, SEQ, LANES), lambda b, i: (b, 0, A_K_SWA // LANES)),
            pl.BlockSpec((1, SEQ, LANES), lambda b, i: (b, 0, A_V_SWA // LANES)),
        ],
        out_specs=pl.BlockSpec((1, TQ_SWA, SWA_WIDTH), lambda b, i: (b, i, 0)),
        out_shape=jax.ShapeDtypeStruct((BATCH, SEQ, SWA_WIDTH), BF16),
        compiler_params=pltpu.CompilerParams(
            dimension_semantics=("arbitrary", "arbitrary"),
            vmem_limit_bytes=VMEM_LIMIT_BYTES),
        name="swa_attn",
    )(sinks, proj_a, proj_a, proj_a)


def _post_kernel(x_ref, od_ref, os_ref, wo_ref, gpost1_ref, gt1_ref, gpre2_ref, sc2_ref, sh2_ref,
                 wg_ref, wu_ref, wd_ref, gpost2_ref, gt2_ref, o_ref):
    def mixer_residual(rows):
        x = x_ref[0, rows, :]
        y = jnp.dot(od_ref[0, rows, :], wo_ref[0:DIFF_WIDTH, :], preferred_element_type=F32)
        y = y + jnp.dot(os_ref[0, rows, :], wo_ref[DIFF_WIDTH:D_MODEL, :], preferred_element_type=F32)
        x1 = x + gt1_ref[0] * (_rms(y) * gpost1_ref[...])
        h = (_rms(x1) * gpre2_ref[...]) * (1.0 + sc2_ref[0]) + sh2_ref[0]
        return x1, h.astype(BF16)

    def ffn_residual(x1, hb):
        gate = jnp.dot(hb, wg_ref[...], preferred_element_type=F32)
        up = jnp.dot(hb, wu_ref[...], preferred_element_type=F32)
        act = (gate / (1.0 + jnp.exp(-gate))) * up
        y2 = jnp.dot(act.astype(BF16), wd_ref[...], preferred_element_type=F32)
        return x1 + gt2_ref[0] * (_rms(y2) * gpost2_ref[...])

    subs = [slice(k * SUB_POST, (k + 1) * SUB_POST) for k in range(TM_POST // SUB_POST)]
    staged = [mixer_residual(rows) for rows in subs]
    for rows, (x1, hb) in zip(subs, staged):
        o_ref[0, rows, :] = ffn_residual(x1, hb)


def _post_call(layer, x, o_diff, o_swa, mod3, w_out_b, g_post1, g_pre2, w_gate_b, w_up_b, w_down_b, g_post2):
    n_t = SEQ // TM_POST
    row = lambda: pl.BlockSpec((1, D_MODEL), lambda b, i: (0, 0))
    return pl.pallas_call(
        _post_kernel,
        grid=(BATCH, n_t),
        in_specs=[
            pl.BlockSpec((1, TM_POST, D_MODEL), lambda b, i: (b, i, 0)),
            pl.BlockSpec((1, TM_POST, DIFF_WIDTH), lambda b, i: (b, i, 0)),
            pl.BlockSpec((1, TM_POST, SWA_WIDTH), lambda b, i: (b, i, 0)),
            _resident((D_MODEL, D_MODEL)),
            row(),
            _mod_spec(layer, 2),
            row(),
            _mod_spec(layer, 4),
            _mod_spec(layer, 3),
            _resident((D_MODEL, D_FF)),
            _resident((D_MODEL, D_FF)),
            _resident((D_FF, D_MODEL)),
            row(),
            _mod_spec(layer, 5),
        ],
        out_specs=pl.BlockSpec((1, TM_POST, D_MODEL), lambda b, i: (b, i, 0)),
        out_shape=jax.ShapeDtypeStruct((BATCH, SEQ, D_MODEL), F32),
        compiler_params=pltpu.CompilerParams(
            dimension_semantics=("arbitrary", "arbitrary"),
            vmem_limit_bytes=VMEM_LIMIT_BYTES),
        name="post_ffn",
    )(x, o_diff, o_swa, w_out_b, g_post1.reshape(1, D_MODEL), mod3, g_pre2.reshape(1, D_MODEL),
      mod3, mod3, w_gate_b, w_up_b, w_down_b, g_post2.reshape(1, D_MODEL), mod3)


def kernel(x, c, ada_w, ada_b, g_mix_pre, g_mix_post, g_ffn_pre, g_ffn_post, w_in, lambda_q1, lambda_k1,
           lambda_q2, lambda_k2, subln_g, sinks, w_out, w_gate, w_up, w_down):
    cols_a, scale_a, cols_t, scale_t = _in_proj_layout()
    out_perm = _out_proj_row_order()
    tables = _rope_tables()

    mod = _ada_call(c, ada_w, ada_b)
    mod3 = mod.reshape(DEPTH * BATCH * 6, 1, D_MODEL)

    for layer in range(DEPTH):
        w_a = (w_in[layer][:, cols_a] * scale_a[None, :]).astype(BF16)
        w_t = (w_in[layer][:, cols_t] * scale_t[None, :]).T.astype(BF16)
        w_out_b = w_out[layer][out_perm, :].astype(BF16)
        lam_vec = jnp.stack([lambda_q1[layer], lambda_k1[layer], lambda_q2[layer], lambda_k2[layer]])

        proj_a, proj_t = _pre_call(layer, x, mod3, g_mix_pre[layer], w_a, w_t, tables)
        o_diff = _diff_call(layer, proj_a, proj_t, lam_vec, subln_g[layer])
        o_swa = _swa_call(proj_a, sinks[layer])
        x = _post_call(layer, x, o_diff, o_swa, mod3, w_out_b, g_mix_post[layer], g_ffn_pre[layer],
                       w_gate[layer].astype(BF16), w_up[layer].astype(BF16), w_down[layer].astype(BF16),
                       g_ffn_post[layer])
    return x
```

```python
import functools
import math

import numpy as np
import jax
import jax.numpy as jnp
from jax import lax
from jax.experimental import pallas as pl
from jax.experimental.pallas import tpu as pltpu

D_MODEL = 1024
BATCH = 16
SEQ = 2048
DEPTH = 2
HEAD_DIM = 64
HALF = HEAD_DIM // 2
DIFF_HEADS = 4
DIFF_WIDTH = 512
SWA_HEADS = 8
SWA_KV_HEADS = 2
SWA_GROUP = SWA_HEADS // SWA_KV_HEADS
SWA_WIDTH = 512
WINDOW = 128
ROPE_THETA = 10000.0
D_FF = 2816
EPS = 1e-6
NEG = -1e30
LOG2E = math.log2(math.e)

LANES = 128

REF_Q_DIFF, REF_K_DIFF, REF_V_DIFF, REF_Q_SWA, REF_K_SWA, REF_V_SWA = 0, 512, 1024, 1536, 2048, 2176
A_K_DIFF, A_K_SWA, A_WIDTH = 0, 512, 640
T_Q_DIFF, T_V_DIFF, T_Q_SWA, T_V_SWA, T_WIDTH = 0, 512, 1024, 1536, 1664
T_ROPE_ROWS = ((T_Q_DIFF, DIFF_WIDTH), (T_Q_SWA, SWA_WIDTH))

VMEM_LIMIT_BYTES = 56 * 1024 * 1024

TM_PRE = 512
TM_POST = 512
SUB_POST = 256
T_DIFF = 256
DENOM_ROWS = 16
TQ_SWA = 512
SWA_PIPE_GROUP = 4

F32 = jnp.float32
BF16 = jnp.bfloat16
NT_DIMS = (((1,), (1,)), ((), ()))


def _in_proj_layout():
    j = np.arange(LANES)
    second, which, d = j // 64, (j // HALF) % 2, j % HALF
    pair_order = which * HEAD_DIM + second * HALF + d
    score_scale = HEAD_DIM ** -0.5

    cols_a = np.zeros(A_WIDTH, np.int64)
    for h in range(DIFF_HEADS):
        cols_a[A_K_DIFF + h * LANES + j] = REF_K_DIFF + h * LANES + pair_order
    cols_a[A_K_SWA + j] = REF_K_SWA + pair_order

    cols_t = np.zeros(T_WIDTH, np.int64)
    scale_t = np.ones(T_WIDTH, np.float32)
    for h in range(DIFF_HEADS):
        cols_t[T_Q_DIFF + h * LANES + j] = REF_Q_DIFF + h * LANES + pair_order
    cols_t[T_V_DIFF:T_V_DIFF + DIFF_WIDTH] = REF_V_DIFF + np.arange(DIFF_WIDTH)
    for g in range(SWA_GROUP):
        cols_t[T_Q_SWA + g * LANES + j] = REF_Q_SWA + (which * SWA_GROUP + g) * HEAD_DIM + second * HALF + d
    cols_t[T_V_SWA:T_V_SWA + LANES] = REF_V_SWA + j
    for start, size in T_ROPE_ROWS:
        scale_t[start:start + size] = score_scale
    return cols_a, cols_t, scale_t


def _out_proj_row_order():
    perm = np.arange(D_MODEL)
    d = np.arange(HEAD_DIM)
    for g in range(SWA_GROUP):
        for t in range(SWA_KV_HEADS):
            perm[DIFF_WIDTH + g * LANES + t * HEAD_DIM + d] = DIFF_WIDTH + (t * SWA_GROUP + g) * HEAD_DIM + d
    return perm


def _rope_tables():
    pos = jnp.arange(SEQ, dtype=F32)
    inv = ROPE_THETA ** (-jnp.arange(0, HEAD_DIM, 2, dtype=F32) / HEAD_DIM)
    ang = pos[:, None] * inv[None, :]
    cos, sin = jnp.cos(ang), jnp.sin(ang)
    cos4 = jnp.concatenate([cos, cos, cos, cos], axis=-1)
    sin4 = jnp.concatenate([-sin, -sin, sin, sin], axis=-1)
    cos_t = jnp.concatenate([cos, cos], axis=-1).T * LOG2E
    sin_t = jnp.concatenate([sin, sin], axis=-1).T * LOG2E
    return cos4, sin4, cos_t, sin_t


def _rms(x):
    return x * lax.rsqrt(jnp.mean(x * x, axis=-1, keepdims=True) + EPS)


def _ada_kernel(c_ref, w_ref, b_ref, o_ref):
    c = c_ref[...]
    c_act = c / (1.0 + jnp.exp(-c))
    o_ref[0] = jnp.dot(c_act, w_ref[0], preferred_element_type=F32) + b_ref[0]


def _ada_call(c, ada_w, ada_b):
    n_chunk = 6
    return pl.pallas_call(
        _ada_kernel,
        grid=(DEPTH, n_chunk),
        in_specs=[
            pl.BlockSpec((BATCH, D_MODEL), lambda l, k: (0, 0)),
            pl.BlockSpec((1, D_MODEL, D_MODEL), lambda l, k: (l, 0, k)),
            pl.BlockSpec((1, 1, D_MODEL), lambda l, k: (l, 0, k)),
        ],
        out_specs=pl.BlockSpec((1, BATCH, D_MODEL), lambda l, k: (l, 0, k)),
        out_shape=jax.ShapeDtypeStruct((DEPTH, BATCH, 6 * D_MODEL), F32),
        name="ada_mod",
    )(c, ada_w, ada_b.reshape(DEPTH, 1, 6 * D_MODEL))


def _pre_kernel(x_ref, g_ref, sh_ref, sc_ref, wa_ref, wt_ref, cos_ref, sin_ref, cost_ref, sint_ref,
                oa_ref, ot_ref):
    x = x_ref[0]
    h = (_rms(x) * g_ref[...]) * (1.0 + sc_ref[0]) + sh_ref[0]
    hb = h.astype(BF16)

    proj = jnp.dot(hb, wa_ref[...], preferred_element_type=F32)
    cos = cos_ref[...]
    sin = sin_ref[...]
    for grp in range(A_WIDTH // LANES):
        cols = slice(grp * LANES, (grp + 1) * LANES)
        xg = proj[:, cols]
        oa_ref[0, :, cols] = (xg * cos + pltpu.roll(xg, 64, 1) * sin).astype(BF16)

    proj_t = lax.dot_general(wt_ref[...], hb, NT_DIMS, preferred_element_type=F32)
    cos_t = cost_ref[...]
    sin_t = sint_ref[...]
    n_blk = TM_PRE // T_DIFF

    def store_t(rows, value):
        for blk in range(n_blk):
            ot_ref[0, blk, rows, :] = value[:, blk * T_DIFF:(blk + 1) * T_DIFF]

    for start, size in T_ROPE_ROWS:
        for grp in range(size // LANES):
            base = start + grp * LANES
            lo = proj_t[base:base + 64]
            hi = proj_t[base + 64:base + LANES]
            store_t(slice(base, base + 64), (lo * cos_t - hi * sin_t).astype(BF16))
            store_t(slice(base + 64, base + LANES), (hi * cos_t + lo * sin_t).astype(BF16))
    store_t(slice(T_V_DIFF, T_V_DIFF + DIFF_WIDTH), proj_t[T_V_DIFF:T_V_DIFF + DIFF_WIDTH].astype(BF16))
    store_t(slice(T_V_SWA, T_V_SWA + LANES), proj_t[T_V_SWA:T_V_SWA + LANES].astype(BF16))


def _mod_spec(layer, which):
    def index(b, i):
        return ((layer * BATCH + b) * 6 + which, 0, 0)
    return pl.BlockSpec((1, 1, D_MODEL), index)


def _resident(shape):
    return pl.BlockSpec(shape, lambda b, i: (0,) * len(shape), pipeline_mode=pl.Buffered(1))


def _pre_call(layer, x, mod3, g_pre, w_a, w_t, tables):
    cos4, sin4, cos_t, sin_t = tables
    n_t = SEQ // TM_PRE
    n_blk = TM_PRE // T_DIFF
    return pl.pallas_call(
        _pre_kernel,
        grid=(BATCH, n_t),
        in_specs=[
            pl.BlockSpec((1, TM_PRE, D_MODEL), lambda b, i: (b, i, 0)),
            pl.BlockSpec((1, D_MODEL), lambda b, i: (0, 0)),
            _mod_spec(layer, 0),
            _mod_spec(layer, 1),
            _resident((D_MODEL, A_WIDTH)),
            _resident((T_WIDTH, D_MODEL)),
            pl.BlockSpec((TM_PRE, LANES), lambda b, i: (i, 0)),
            pl.BlockSpec((TM_PRE, LANES), lambda b, i: (i, 0)),
            pl.BlockSpec((HEAD_DIM, TM_PRE), lambda b, i: (0, i)),
            pl.BlockSpec((HEAD_DIM, TM_PRE), lambda b, i: (0, i)),
        ],
        out_specs=[
            pl.BlockSpec((1, TM_PRE, A_WIDTH), lambda b, i: (b, i, 0)),
            pl.BlockSpec((1, n_blk, T_WIDTH, T_DIFF), lambda b, i: (b, i, 0, 0)),
        ],
        out_shape=[
            jax.ShapeDtypeStruct((BATCH, SEQ, A_WIDTH), BF16),
            jax.ShapeDtypeStruct((BATCH, SEQ // T_DIFF, T_WIDTH, T_DIFF), BF16),
        ],
        compiler_params=pltpu.CompilerParams(
            dimension_semantics=("arbitrary", "arbitrary"),
            vmem_limit_bytes=VMEM_LIMIT_BYTES),
        name="pre_mixer",
    )(x, g_pre.reshape(1, D_MODEL), mod3, mod3, w_a, w_t, cos4, sin4, cos_t, sin_t)


def _diff_kernel(lam_ref, g_ref, qt_ref, k_ref, vt_ref, o_ref,
                 qbd_ref, m_ref, acc_ref, s0_ref, s1_ref, max0_ref, max1_ref, *, lam_init):
    pair_idx = pl.program_id(1)
    t = T_DIFF
    feat = lax.broadcasted_iota(jnp.int32, (LANES, 1), 0)
    first_map = ((feat // HALF) % 2) == 0

    chains = [(sub, h) for sub in range(2) for h in range(DIFF_HEADS)]
    sub_chains = lambda sub: [c for c, (sb, _) in enumerate(chains) if sb == sub]
    all_chains = list(range(len(chains)))

    for c, (sub, h) in enumerate(chains):
        qh = qt_ref[0, sub, h * LANES:(h + 1) * LANES, :]
        zero = jnp.zeros_like(qh)
        qbd_ref[c, :, 0:t] = jnp.where(first_map, qh, zero)
        qbd_ref[c, :, t:2 * t] = jnp.where(first_map, zero, qh)
        m_ref[c] = jnp.full((1, 2 * t), NEG, F32)
        acc_ref[c] = jnp.zeros((LANES + DENOM_ROWS, 2 * t), F32)
    ones_rows = jnp.ones((DENOM_ROWS, t), BF16)

    def scores(j, c, s_ref, max_ref):
        h = chains[c][1]
        start = pl.multiple_of(j * t, t)
        kh = k_ref[0, pl.ds(start, t), h * LANES:(h + 1) * LANES]
        s = jnp.dot(kh, qbd_ref[c], preferred_element_type=F32)
        s_ref[c] = s
        max_ref[c] = jnp.max(s, axis=0, keepdims=True)

    def accumulate(j, c, s_ref, max_ref, diagonal):
        h = chains[c][1]
        vth = vt_ref[0, j, h * LANES:(h + 1) * LANES, :]
        s = s_ref[c]
        if diagonal:
            key = lax.broadcasted_iota(jnp.int32, (t, 2 * t), 0)
            qry = lax.broadcasted_iota(jnp.int32, (t, 2 * t), 1) & (t - 1)
            s = jnp.where(key <= qry, s, NEG)
            m_blk = jnp.max(s, axis=0, keepdims=True)
        else:
            m_blk = max_ref[c]
        m_old = m_ref[c]
        m_new = jnp.maximum(m_old, m_blk)
        p = jnp.exp2(s - m_new)
        alpha = jnp.exp2(m_old - m_new)
        v_aug = jnp.concatenate([vth, ones_rows], axis=0)
        acc_ref[c] = alpha * acc_ref[c] + jnp.dot(v_aug, p.astype(BF16), preferred_element_type=F32)
        m_ref[c] = m_new

    for c in all_chains:
        scores(0, c, s0_ref, max0_ref)

    def two_blocks(step, carry):
        j = 2 * step
        for c in all_chains:
            scores(j + 1, c, s1_ref, max1_ref)
            accumulate(j, c, s0_ref, max0_ref, False)
        for c in all_chains:
            scores(j + 2, c, s0_ref, max0_ref)
            accumulate(j + 1, c, s1_ref, max1_ref, False)
        return carry

    lax.fori_loop(0, pair_idx, two_blocks, 0)

    diag0 = 2 * pair_idx
    for c0, c1 in zip(sub_chains(0), sub_chains(1)):
        scores(diag0 + 1, c1, s1_ref, max1_ref)
        accumulate(diag0, c0, s0_ref, max0_ref, True)
        accumulate(diag0, c1, s0_ref, max0_ref, False)
    for c1 in sub_chains(1):
        accumulate(diag0 + 1, c1, s1_ref, max1_ref, True)

    lam_vec = lam_ref[...]
    dot1 = jnp.sum(lam_vec[0:1] * lam_vec[1:2], axis=-1, keepdims=True)
    dot2 = jnp.sum(lam_vec[2:3] * lam_vec[3:4], axis=-1, keepdims=True)
    lam = jnp.exp(dot1) - jnp.exp(dot2) + lam_init
    g = g_ref[...]
    for c, (sub, h) in enumerate(chains):
        denom = acc_ref[c, LANES:LANES + 1, :]
        o_all = acc_ref[c, 0:LANES, :] * (1.0 / denom)
        o = o_all[:, 0:t] - lam * o_all[:, t:2 * t]
        o = o * lax.rsqrt(jnp.mean(o * o, axis=0, keepdims=True) + EPS)
        o = (o * g) * (1.0 - lam_init)
        o_ref[0, sub * t:(sub + 1) * t, h * LANES:(h + 1) * LANES] = o.T.astype(BF16)


def _diff_call(layer, proj_a, proj_t, lam_vec, subln_g):
    lam_init = 0.8 - 0.6 * math.exp(-0.3 * layer)
    n_blocks = SEQ // T_DIFF
    n_chains = 2 * DIFF_HEADS
    kernel = functools.partial(_diff_kernel, lam_init=lam_init)
    return pl.pallas_call(
        kernel,
        grid=(BATCH, n_blocks // 2),
        in_specs=[
            pl.BlockSpec((4, HEAD_DIM), lambda b, i: (0, 0)),
            pl.BlockSpec((LANES, 1), lambda b, i: (0, 0)),
            pl.BlockSpec((1, 2, DIFF_WIDTH, T_DIFF), lambda b, i: (b, i, T_Q_DIFF // DIFF_WIDTH, 0)),
            pl.BlockSpec((1, SEQ, DIFF_WIDTH), lambda b, i: (b, 0, A_K_DIFF // DIFF_WIDTH)),
            pl.BlockSpec((1, n_blocks, DIFF_WIDTH, T_DIFF), lambda b, i: (b, 0, T_V_DIFF // DIFF_WIDTH, 0)),
        ],
        out_specs=pl.BlockSpec((1, 2 * T_DIFF, DIFF_WIDTH), lambda b, i: (b, i, 0)),
        out_shape=jax.ShapeDtypeStruct((BATCH, SEQ, DIFF_WIDTH), BF16),
        scratch_shapes=[
            pltpu.VMEM((n_chains, LANES, 2 * T_DIFF), BF16),
            pltpu.VMEM((n_chains, 1, 2 * T_DIFF), F32),
            pltpu.VMEM((n_chains, LANES + DENOM_ROWS, 2 * T_DIFF), F32),
            pltpu.VMEM((n_chains, T_DIFF, 2 * T_DIFF), F32),
            pltpu.VMEM((n_chains, T_DIFF, 2 * T_DIFF), F32),
            pltpu.VMEM((n_chains, 1, 2 * T_DIFF), F32),
            pltpu.VMEM((n_chains, 1, 2 * T_DIFF), F32),
        ],
        compiler_params=pltpu.CompilerParams(
            dimension_semantics=("arbitrary", "arbitrary"),
            vmem_limit_bytes=VMEM_LIMIT_BYTES),
        name="diff_attn",
    )(lam_vec, subln_g.reshape(LANES, 1), proj_t, proj_a, proj_t)


def _swa_kernel(sink_ref, qt_ref, kprev_ref, kcur_ref, vtprev_ref, vtcur_ref, o_ref,
                s_ref, pv_ref, m_ref, bias_ref):
    i = pl.program_id(1)
    w = WINDOW
    n_sub = TQ_SWA // w
    feat = lax.broadcasted_iota(jnp.int32, (LANES, 1), 0)
    kv0_feat = ((feat // HALF) % 2) == 0
    col = lax.broadcasted_iota(jnp.int32, (1, 2 * w), 1)
    first_cols = col < w

    @pl.when(i == 0)
    def _():
        key = lax.broadcasted_iota(jnp.int32, (2 * w, 2 * w), 0)
        qry = lax.broadcasted_iota(jnp.int32, (2 * w, 2 * w), 1) & (w - 1)
        in_window = (key > qry) & (key <= qry + w)
        bias_ref[0] = jnp.where(in_window, 0.0, NEG)
        bias_ref[1] = jnp.where(in_window & (key >= w), 0.0, NEG)
    ones_rows = jnp.ones((DENOM_ROWS, 2 * w), BF16)

    def band(n):
        tile, half = divmod(n, T_DIFF // w)
        if n == 0:
            k_band = jnp.concatenate([kprev_ref[0], kcur_ref[0, 0:w, :]], axis=0)
            vt_band = jnp.concatenate([vtprev_ref[0, 0, :, w:2 * w], vtcur_ref[0, 0, :, 0:w]], axis=1)
            return k_band, vt_band, bias_ref[jnp.where(i == 0, 1, 0)]
        k_band = kcur_ref[0, (n - 1) * w:(n + 1) * w, :]
        if half == 1:
            vt_band = vtcur_ref[0, tile]
        else:
            vt_band = jnp.concatenate([vtcur_ref[0, tile - 1, :, w:2 * w], vtcur_ref[0, tile, :, 0:w]], axis=1)
        return k_band, vt_band, bias_ref[0]

    def scores(n, g):
        tile, half = divmod(n, T_DIFF // w)
        k_band, _, bias = band(n)
        qg = qt_ref[0, tile, g * LANES:(g + 1) * LANES, half * w:(half + 1) * w]
        zero = jnp.zeros_like(qg)
        qbd = jnp.concatenate([jnp.where(kv0_feat, qg, zero), jnp.where(kv0_feat, zero, qg)], axis=1)
        s = jnp.dot(k_band, qbd, preferred_element_type=F32)
        s_ref[n * SWA_GROUP + g] = s + bias

    def sink_row(g):
        return jnp.where(first_cols, sink_ref[g], sink_ref[SWA_GROUP + g]) * LOG2E

    def attend(n, g):
        c = n * SWA_GROUP + g
        _, vt_band, _ = band(n)
        v_aug = jnp.concatenate([vt_band, ones_rows], axis=0)
        s = s_ref[c]
        m = jnp.maximum(jnp.max(s, axis=0, keepdims=True), sink_row(g))
        p = jnp.exp2(s - m)
        pv_ref[c] = jnp.dot(v_aug, p.astype(BF16), preferred_element_type=F32)
        m_ref[c] = m

    def finish(n, g):
        c = n * SWA_GROUP + g
        pv = pv_ref[c]
        inv = 1.0 / (pv[LANES:LANES + 1, :] + jnp.exp2(sink_row(g) - m_ref[c]))
        o_pair = jnp.concatenate([pv[0:HEAD_DIM, 0:w] * inv[:, 0:w],
                                  pv[HEAD_DIM:LANES, w:2 * w] * inv[:, w:2 * w]], axis=0)
        o_ref[0, n * w:(n + 1) * w, g * LANES:(g + 1) * LANES] = o_pair.T.astype(BF16)

    order = [(n, g) for n in range(n_sub) for g in range(SWA_GROUP)]
    stages = (scores, attend, finish)
    group = SWA_PIPE_GROUP
    groups = [order[k:k + group] for k in range(0, len(order), group)]
    for tick in range(len(groups) + len(stages) - 1):
        for lag, stage in enumerate(stages):
            if 0 <= tick - lag < len(groups):
                for chain in groups[tick - lag]:
                    stage(*chain)


def _swa_call(proj_a, proj_t, sinks):
    n_q = SEQ // TQ_SWA
    tiles = TQ_SWA // T_DIFF
    subs = TQ_SWA // WINDOW
    return pl.pallas_call(
        _swa_kernel,
        grid=(BATCH, n_q),
        in_specs=[
            pl.BlockSpec(memory_space=pltpu.SMEM),
            pl.BlockSpec((1, tiles, SWA_WIDTH, T_DIFF), lambda b, i: (b, i, T_Q_SWA // SWA_WIDTH, 0)),
            pl.BlockSpec((1, WINDOW, LANES), lambda b, i: (b, jnp.maximum(i * subs - 1, 0), A_K_SWA // LANES)),
            pl.BlockSpec((1, TQ_SWA, LANES), lambda b, i: (b, i, A_K_SWA // LANES)),
            pl.BlockSpec((1, 1, LANES, T_DIFF), lambda b, i: (b, jnp.maximum(i * tiles - 1, 0), T_V_SWA // LANES, 0)),
            pl.BlockSpec((1, tiles, LANES, T_DIFF), lambda b, i: (b, i, T_V_SWA // LANES, 0)),
        ],
        out_specs=pl.BlockSpec((1, TQ_SWA, SWA_WIDTH), lambda b, i: (b, i, 0)),
        out_shape=jax.ShapeDtypeStruct((BATCH, SEQ, SWA_WIDTH), BF16),
        scratch_shapes=[
            pltpu.VMEM((subs * SWA_GROUP, 2 * WINDOW, 2 * WINDOW), F32),
            pltpu.VMEM((subs * SWA_GROUP, LANES + DENOM_ROWS, 2 * WINDOW), F32),
            pltpu.VMEM((subs * SWA_GROUP, 1, 2 * WINDOW), F32),
            pltpu.VMEM((2, 2 * WINDOW, 2 * WINDOW), F32),
        ],
        compiler_params=pltpu.CompilerParams(
            dimension_semantics=("arbitrary", "arbitrary"),
            vmem_limit_bytes=VMEM_LIMIT_BYTES),
        name="swa_attn",
    )(sinks, proj_t, proj_a, proj_a, proj_t, proj_t)


def _post_kernel(x_ref, od_ref, os_ref, wo_ref, gpost1_ref, gt1_ref, gpre2_ref, sc2_ref, sh2_ref,
                 wg_ref, wu_ref, wd_ref, gpost2_ref, gt2_ref, o_ref):
    def mixer_residual(rows):
        x = x_ref[0, rows, :]
        y = jnp.dot(od_ref[0, rows, :], wo_ref[0:DIFF_WIDTH, :], preferred_element_type=F32)
        y = y + jnp.dot(os_ref[0, rows, :], wo_ref[DIFF_WIDTH:D_MODEL, :], preferred_element_type=F32)
        x1 = x + gt1_ref[0] * (_rms(y) * gpost1_ref[...])
        h = (_rms(x1) * gpre2_ref[...]) * (1.0 + sc2_ref[0]) + sh2_ref[0]
        return x1, h.astype(BF16)

    def ffn_residual(x1, hb):
        gate = jnp.dot(hb, wg_ref[...], preferred_element_type=F32)
        up = jnp.dot(hb, wu_ref[...], preferred_element_type=F32)
        act = (gate / (1.0 + jnp.exp(-gate))) * up
        y2 = jnp.dot(act.astype(BF16), wd_ref[...], preferred_element_type=F32)
        return x1 + gt2_ref[0] * (_rms(y2) * gpost2_ref[...])

    subs = [slice(k * SUB_POST, (k + 1) * SUB_POST) for k in range(TM_POST // SUB_POST)]
    staged = [mixer_residual(rows) for rows in subs]
    for rows, (x1, hb) in zip(subs, staged):
        o_ref[0, rows, :] = ffn_residual(x1, hb)


def _post_call(layer, x, o_diff, o_swa, mod3, w_out_b, g_post1, g_pre2, w_gate_b, w_up_b, w_down_b, g_post2):
    n_t = SEQ // TM_POST
    row = lambda: pl.BlockSpec((1, D_MODEL), lambda b, i: (0, 0))
    return pl.pallas_call(
        _post_kernel,
        grid=(BATCH, n_t),
        in_specs=[
            pl.BlockSpec((1, TM_POST, D_MODEL), lambda b, i: (b, i, 0)),
            pl.BlockSpec((1, TM_POST, DIFF_WIDTH), lambda b, i: (b, i, 0)),
            pl.BlockSpec((1, TM_POST, SWA_WIDTH), lambda b, i: (b, i, 0)),
            _resident((D_MODEL, D_MODEL)),
            row(),
            _mod_spec(layer, 2),
            row(),
            _mod_spec(layer, 4),
            _mod_spec(layer, 3),
            _resident((D_MODEL, D_FF)),
            _resident((D_MODEL, D_FF)),
            _resident((D_FF, D_MODEL)),
            row(),
            _mod_spec(layer, 5),
        ],
        out_specs=pl.BlockSpec((1, TM_POST, D_MODEL), lambda b, i: (b, i, 0)),
        out_shape=jax.ShapeDtypeStruct((BATCH, SEQ, D_MODEL), F32),
        compiler_params=pltpu.CompilerParams(
            dimension_semantics=("arbitrary", "arbitrary"),
            vmem_limit_bytes=VMEM_LIMIT_BYTES),
        name="post_ffn",
    )(x, o_diff, o_swa, w_out_b, g_post1.reshape(1, D_MODEL), mod3, g_pre2.reshape(1, D_MODEL),
      mod3, mod3, w_gate_b, w_up_b, w_down_b, g_post2.reshape(1, D_MODEL), mod3)


def kernel(x, c, ada_w, ada_b, g_mix_pre, g_mix_post, g_ffn_pre, g_ffn_post, w_in, lambda_q1, lambda_k1,
           lambda_q2, lambda_k2, subln_g, sinks, w_out, w_gate, w_up, w_down):
    cols_a, cols_t, scale_t = _in_proj_layout()
    out_perm = _out_proj_row_order()
    tables = _rope_tables()

    mod = _ada_call(c, ada_w, ada_b)
    mod3 = mod.reshape(DEPTH * BATCH * 6, 1, D_MODEL)

    for layer in range(DEPTH):
        w_a = w_in[layer][:, cols_a].astype(BF16)
        w_t = (w_in[layer][:, cols_t] * scale_t[None, :]).T.astype(BF16)
        w_out_b = w_out[layer][out_perm, :].astype(BF16)
        lam_vec = jnp.stack([lambda_q1[layer], lambda_k1[layer], lambda_q2[layer], lambda_k2[layer]])

        proj_a, proj_t = _pre_call(layer, x, mod3, g_mix_pre[layer], w_a, w_t, tables)
        o_diff = _diff_call(layer, proj_a, proj_t, lam_vec, subln_g[layer])
        o_swa = _swa_call(proj_a, proj_t, sinks[layer])
        x = _post_call(layer, x, o_diff, o_swa, mod3, w_out_b, g_mix_post[layer], g_ffn_pre[layer],
                       w_gate[layer].astype(BF16), w_up[layer].astype(BF16), w_down[layer].astype(BF16),
                       g_ffn_post[layer])
    return x
```

```python
import functools
import math

import numpy as np
import jax
import jax.numpy as jnp
from jax import lax
from jax.experimental import pallas as pl
from jax.experimental.pallas import tpu as pltpu

D_MODEL = 1024
BATCH = 16
SEQ = 2048
DEPTH = 2
HEAD_DIM = 64
HALF = HEAD_DIM // 2
DIFF_HEADS = 4
DIFF_WIDTH = 512
SWA_HEADS = 8
SWA_KV_HEADS = 2
SWA_GROUP = SWA_HEADS // SWA_KV_HEADS
SWA_WIDTH = 512
WINDOW = 128
ROPE_THETA = 10000.0
D_FF = 2816
EPS = 1e-6
NEG = -1e30
LOG2E = math.log2(math.e)

LANES = 128

REF_Q_DIFF, REF_K_DIFF, REF_V_DIFF, REF_Q_SWA, REF_K_SWA, REF_V_SWA = 0, 512, 1024, 1536, 2048, 2176
A_K_DIFF, A_K_SWA, A_WIDTH = 0, 512, 640
T_Q_DIFF, T_V_DIFF, T_Q_SWA, T_V_SWA, T_WIDTH = 0, 512, 1024, 1536, 1664
T_ROPE_ROWS = ((T_Q_DIFF, DIFF_WIDTH), (T_Q_SWA, SWA_WIDTH))

VMEM_LIMIT_BYTES = 56 * 1024 * 1024

TM_PRE = 512
TM_POST = 1024
SUB_POST = 256
T_DIFF = 256
DENOM_ROWS = 16
TQ_SWA = 512
SWA_PIPE_GROUP = 4

F32 = jnp.float32
BF16 = jnp.bfloat16
NT_DIMS = (((1,), (1,)), ((), ()))


def _in_proj_layout():
    j = np.arange(LANES)
    second, which, d = j // 64, (j // HALF) % 2, j % HALF
    pair_order = which * HEAD_DIM + second * HALF + d
    score_scale = HEAD_DIM ** -0.5

    cols_a = np.zeros(A_WIDTH, np.int64)
    for h in range(DIFF_HEADS):
        cols_a[A_K_DIFF + h * LANES + j] = REF_K_DIFF + h * LANES + pair_order
    cols_a[A_K_SWA + j] = REF_K_SWA + pair_order

    cols_t = np.zeros(T_WIDTH, np.int64)
    scale_t = np.ones(T_WIDTH, np.float32)
    for h in range(DIFF_HEADS):
        cols_t[T_Q_DIFF + h * LANES + j] = REF_Q_DIFF + h * LANES + pair_order
    cols_t[T_V_DIFF:T_V_DIFF + DIFF_WIDTH] = REF_V_DIFF + np.arange(DIFF_WIDTH)
    for g in range(SWA_GROUP):
        cols_t[T_Q_SWA + g * LANES + j] = REF_Q_SWA + (which * SWA_GROUP + g) * HEAD_DIM + second * HALF + d
    cols_t[T_V_SWA:T_V_SWA + LANES] = REF_V_SWA + j
    for start, size in T_ROPE_ROWS:
        scale_t[start:start + size] = score_scale
    return cols_a, cols_t, scale_t


def _out_proj_row_order():
    perm = np.arange(D_MODEL)
    d = np.arange(HEAD_DIM)
    for g in range(SWA_GROUP):
        for t in range(SWA_KV_HEADS):
            perm[DIFF_WIDTH + g * LANES + t * HEAD_DIM + d] = DIFF_WIDTH + (t * SWA_GROUP + g) * HEAD_DIM + d
    return perm


def _rope_tables():
    pos = jnp.arange(SEQ, dtype=F32)
    inv = ROPE_THETA ** (-jnp.arange(0, HEAD_DIM, 2, dtype=F32) / HEAD_DIM)
    ang = pos[:, None] * inv[None, :]
    cos, sin = jnp.cos(ang), jnp.sin(ang)
    cos4 = jnp.concatenate([cos, cos, cos, cos], axis=-1)
    sin4 = jnp.concatenate([-sin, -sin, sin, sin], axis=-1)
    cos_t = jnp.concatenate([cos, cos], axis=-1).T * LOG2E
    sin_t = jnp.concatenate([sin, sin], axis=-1).T * LOG2E
    return cos4, sin4, cos_t, sin_t


def _rms(x):
    return x * lax.rsqrt(jnp.mean(x * x, axis=-1, keepdims=True) + EPS)


def _ada_kernel(c_ref, w_ref, b_ref, o_ref):
    c = c_ref[...]
    c_act = c / (1.0 + jnp.exp(-c))
    o_ref[0] = jnp.dot(c_act, w_ref[0], preferred_element_type=F32) + b_ref[0]


def _ada_call(c, ada_w, ada_b):
    n_chunk = 6
    return pl.pallas_call(
        _ada_kernel,
        grid=(DEPTH, n_chunk),
        in_specs=[
            pl.BlockSpec((BATCH, D_MODEL), lambda l, k: (0, 0)),
            pl.BlockSpec((1, D_MODEL, D_MODEL), lambda l, k: (l, 0, k)),
            pl.BlockSpec((1, 1, D_MODEL), lambda l, k: (l, 0, k)),
        ],
        out_specs=pl.BlockSpec((1, BATCH, D_MODEL), lambda l, k: (l, 0, k)),
        out_shape=jax.ShapeDtypeStruct((DEPTH, BATCH, 6 * D_MODEL), F32),
        name="ada_mod",
    )(c, ada_w, ada_b.reshape(DEPTH, 1, 6 * D_MODEL))


def _pre_kernel(x_ref, g_ref, sh_ref, sc_ref, wa_ref, wt_ref, cos_ref, sin_ref, cost_ref, sint_ref,
                oa_ref, ot_ref):
    def modulate(tok):
        x = x_ref[0, tok, :]
        h = (_rms(x) * g_ref[...]) * (1.0 + sc_ref[0]) + sh_ref[0]
        return h.astype(BF16)

    def project(blk, tok, hb):
        proj = jnp.dot(hb, wa_ref[0], preferred_element_type=F32)
        cos = cos_ref[tok, :]
        sin = sin_ref[tok, :]
        for grp in range(A_WIDTH // LANES):
            cols = slice(grp * LANES, (grp + 1) * LANES)
            xg = proj[:, cols]
            oa_ref[0, tok, cols] = (xg * cos + pltpu.roll(xg, 64, 1) * sin).astype(BF16)

        proj_t = lax.dot_general(wt_ref[0], hb, NT_DIMS, preferred_element_type=F32)
        cos_t = cost_ref[:, tok]
        sin_t = sint_ref[:, tok]
        for start, size in T_ROPE_ROWS:
            for grp in range(size // LANES):
                base = start + grp * LANES
                lo = proj_t[base:base + 64]
                hi = proj_t[base + 64:base + LANES]
                ot_ref[0, blk, base:base + 64, :] = (lo * cos_t - hi * sin_t).astype(BF16)
                ot_ref[0, blk, base + 64:base + LANES, :] = (hi * cos_t + lo * sin_t).astype(BF16)
        for start, size in ((T_V_DIFF, DIFF_WIDTH), (T_V_SWA, LANES)):
            ot_ref[0, blk, start:start + size, :] = proj_t[start:start + size].astype(BF16)

    toks = [slice(blk * T_DIFF, (blk + 1) * T_DIFF) for blk in range(TM_PRE // T_DIFF)]
    staged = [modulate(tok) for tok in toks]
    for blk, (tok, hb) in enumerate(zip(toks, staged)):
        project(blk, tok, hb)


def _mod_spec(layer, which):
    def index(b, i):
        return ((layer * BATCH + b) * 6 + which, 0, 0)
    return pl.BlockSpec((1, 1, D_MODEL), index)


def _layer_weight(layer, shape):
    return pl.BlockSpec((1,) + shape, lambda b, i: (layer,) + (0,) * len(shape), pipeline_mode=pl.Buffered(1))


def _pre_call(layer, x, mod3, g_pre, w_a, w_t, tables):
    cos4, sin4, cos_t, sin_t = tables
    n_t = SEQ // TM_PRE
    n_blk = TM_PRE // T_DIFF
    return pl.pallas_call(
        _pre_kernel,
        grid=(BATCH, n_t),
        in_specs=[
            pl.BlockSpec((1, TM_PRE, D_MODEL), lambda b, i: (b, i, 0)),
            pl.BlockSpec((1, D_MODEL), lambda b, i: (0, 0)),
            _mod_spec(layer, 0),
            _mod_spec(layer, 1),
            _layer_weight(layer, (D_MODEL, A_WIDTH)),
            _layer_weight(layer, (T_WIDTH, D_MODEL)),
            pl.BlockSpec((TM_PRE, LANES), lambda b, i: (i, 0)),
            pl.BlockSpec((TM_PRE, LANES), lambda b, i: (i, 0)),
            pl.BlockSpec((HEAD_DIM, TM_PRE), lambda b, i: (0, i)),
            pl.BlockSpec((HEAD_DIM, TM_PRE), lambda b, i: (0, i)),
        ],
        out_specs=[
            pl.BlockSpec((1, TM_PRE, A_WIDTH), lambda b, i: (b, i, 0)),
            pl.BlockSpec((1, n_blk, T_WIDTH, T_DIFF), lambda b, i: (b, i, 0, 0)),
        ],
        out_shape=[
            jax.ShapeDtypeStruct((BATCH, SEQ, A_WIDTH), BF16),
            jax.ShapeDtypeStruct((BATCH, SEQ // T_DIFF, T_WIDTH, T_DIFF), BF16),
        ],
        compiler_params=pltpu.CompilerParams(
            dimension_semantics=("arbitrary", "arbitrary"),
            vmem_limit_bytes=VMEM_LIMIT_BYTES),
        name="pre_mixer",
    )(x, g_pre.reshape(1, D_MODEL), mod3, mod3, w_a, w_t, cos4, sin4, cos_t, sin_t)


def _diff_kernel(lam_ref, g_ref, qt_ref, k_ref, vt_ref, o_ref,
                 qbd_ref, m_ref, acc_ref, s0_ref, s1_ref, max0_ref, max1_ref, *, lam_init):
    pair_idx = pl.program_id(1)
    t = T_DIFF
    feat = lax.broadcasted_iota(jnp.int32, (LANES, 1), 0)
    first_map = ((feat // HALF) % 2) == 0

    chains = [(sub, h) for sub in range(2) for h in range(DIFF_HEADS)]
    sub_chains = lambda sub: [c for c, (sb, _) in enumerate(chains) if sb == sub]
    all_chains = list(range(len(chains)))

    for c, (sub, h) in enumerate(chains):
        qh = qt_ref[0, sub, h * LANES:(h + 1) * LANES, :]
        zero = jnp.zeros_like(qh)
        qbd_ref[c, :, 0:t] = jnp.where(first_map, qh, zero)
        qbd_ref[c, :, t:2 * t] = jnp.where(first_map, zero, qh)
        m_ref[c] = jnp.full((1, 2 * t), NEG, F32)
        acc_ref[c] = jnp.zeros((LANES + DENOM_ROWS, 2 * t), F32)
    ones_rows = jnp.ones((DENOM_ROWS, t), BF16)

    def scores(j, c, s_ref, max_ref):
        h = chains[c][1]
        start = pl.multiple_of(j * t, t)
        kh = k_ref[0, pl.ds(start, t), h * LANES:(h + 1) * LANES]
        s = jnp.dot(kh, qbd_ref[c], preferred_element_type=F32)
        s_ref[c] = s
        max_ref[c] = jnp.max(s, axis=0, keepdims=True)

    def accumulate(j, c, s_ref, max_ref, diagonal):
        h = chains[c][1]
        vth = vt_ref[0, j, h * LANES:(h + 1) * LANES, :]
        s = s_ref[c]
        if diagonal:
            key = lax.broadcasted_iota(jnp.int32, (t, 2 * t), 0)
            qry = lax.broadcasted_iota(jnp.int32, (t, 2 * t), 1) & (t - 1)
            s = jnp.where(key <= qry, s, NEG)
            m_blk = jnp.max(s, axis=0, keepdims=True)
        else:
            m_blk = max_ref[c]
        m_old = m_ref[c]
        m_new = jnp.maximum(m_old, m_blk)
        p = jnp.exp2(s - m_new)
        alpha = jnp.exp2(m_old - m_new)
        v_aug = jnp.concatenate([vth, ones_rows], axis=0)
        acc_ref[c] = alpha * acc_ref[c] + jnp.dot(v_aug, p.astype(BF16), preferred_element_type=F32)
        m_ref[c] = m_new

    for c in all_chains:
        scores(0, c, s0_ref, max0_ref)

    def two_blocks(step, carry):
        j = 2 * step
        for c in all_chains:
            scores(j + 1, c, s1_ref, max1_ref)
            accumulate(j, c, s0_ref, max0_ref, False)
        for c in all_chains:
            scores(j + 2, c, s0_ref, max0_ref)
            accumulate(j + 1, c, s1_ref, max1_ref, False)
        return carry

    lax.fori_loop(0, pair_idx, two_blocks, 0)

    diag0 = 2 * pair_idx
    for c0, c1 in zip(sub_chains(0), sub_chains(1)):
        scores(diag0 + 1, c1, s1_ref, max1_ref)
        accumulate(diag0, c0, s0_ref, max0_ref, True)
        accumulate(diag0, c1, s0_ref, max0_ref, False)
    for c1 in sub_chains(1):
        accumulate(diag0 + 1, c1, s1_ref, max1_ref, True)

    lam_vec = lam_ref[...]
    dot1 = jnp.sum(lam_vec[0:1] * lam_vec[1:2], axis=-1, keepdims=True)
    dot2 = jnp.sum(lam_vec[2:3] * lam_vec[3:4], axis=-1, keepdims=True)
    lam = jnp.exp(dot1) - jnp.exp(dot2) + lam_init
    g = g_ref[...]
    for c, (sub, h) in enumerate(chains):
        denom = acc_ref[c, LANES:LANES + 1, :]
        o_all = acc_ref[c, 0:LANES, :] * (1.0 / denom)
        o = o_all[:, 0:t] - lam * o_all[:, t:2 * t]
        o = o * lax.rsqrt(jnp.mean(o * o, axis=0, keepdims=True) + EPS)
        o = (o * g) * (1.0 - lam_init)
        o_ref[0, sub * t:(sub + 1) * t, h * LANES:(h + 1) * LANES] = o.T.astype(BF16)


def _diff_call(layer, proj_a, proj_t, lam_vec, subln_g):
    lam_init = 0.8 - 0.6 * math.exp(-0.3 * layer)
    n_blocks = SEQ // T_DIFF
    n_chains = 2 * DIFF_HEADS
    kernel = functools.partial(_diff_kernel, lam_init=lam_init)
    return pl.pallas_call(
        kernel,
        grid=(BATCH, n_blocks // 2),
        in_specs=[
            pl.BlockSpec((4, HEAD_DIM), lambda b, i: (0, 0)),
            pl.BlockSpec((LANES, 1), lambda b, i: (0, 0)),
            pl.BlockSpec((1, 2, DIFF_WIDTH, T_DIFF), lambda b, i: (b, i, T_Q_DIFF // DIFF_WIDTH, 0)),
            pl.BlockSpec((1, SEQ, DIFF_WIDTH), lambda b, i: (b, 0, A_K_DIFF // DIFF_WIDTH)),
            pl.BlockSpec((1, n_blocks, DIFF_WIDTH, T_DIFF), lambda b, i: (b, 0, T_V_DIFF // DIFF_WIDTH, 0)),
        ],
        out_specs=pl.BlockSpec((1, 2 * T_DIFF, DIFF_WIDTH), lambda b, i: (b, i, 0)),
        out_shape=jax.ShapeDtypeStruct((BATCH, SEQ, DIFF_WIDTH), BF16),
        scratch_shapes=[
            pltpu.VMEM((n_chains, LANES, 2 * T_DIFF), BF16),
            pltpu.VMEM((n_chains, 1, 2 * T_DIFF), F32),
            pltpu.VMEM((n_chains, LANES + DENOM_ROWS, 2 * T_DIFF), F32),
            pltpu.VMEM((n_chains, T_DIFF, 2 * T_DIFF), F32),
            pltpu.VMEM((n_chains, T_DIFF, 2 * T_DIFF), F32),
            pltpu.VMEM((n_chains, 1, 2 * T_DIFF), F32),
            pltpu.VMEM((n_chains, 1, 2 * T_DIFF), F32),
        ],
        compiler_params=pltpu.CompilerParams(
            dimension_semantics=("arbitrary", "arbitrary"),
            vmem_limit_bytes=VMEM_LIMIT_BYTES),
        name="diff_attn",
    )(lam_vec, subln_g.reshape(LANES, 1), proj_t, proj_a, proj_t)


def _swa_kernel(sink_ref, qt_ref, kprev_ref, kcur_ref, vtprev_ref, vtcur_ref, o_ref,
                s_ref, pv_ref, m_ref, bias_ref):
    i = pl.program_id(1)
    w = WINDOW
    n_sub = TQ_SWA // w
    feat = lax.broadcasted_iota(jnp.int32, (LANES, 1), 0)
    kv0_feat = ((feat // HALF) % 2) == 0
    col = lax.broadcasted_iota(jnp.int32, (1, 2 * w), 1)
    first_cols = col < w

    @pl.when(i == 0)
    def _():
        key = lax.broadcasted_iota(jnp.int32, (2 * w, 2 * w), 0)
        qry = lax.broadcasted_iota(jnp.int32, (2 * w, 2 * w), 1) & (w - 1)
        in_window = (key > qry) & (key <= qry + w)
        bias_ref[0] = jnp.where(in_window, 0.0, NEG)
        bias_ref[1] = jnp.where(in_window & (key >= w), 0.0, NEG)
    ones_rows = jnp.ones((DENOM_ROWS, 2 * w), BF16)

    def band(n):
        tile, half = divmod(n, T_DIFF // w)
        if n == 0:
            k_band = jnp.concatenate([kprev_ref[0], kcur_ref[0, 0:w, :]], axis=0)
            vt_band = jnp.concatenate([vtprev_ref[0, 0, :, w:2 * w], vtcur_ref[0, 0, :, 0:w]], axis=1)
            return k_band, vt_band, bias_ref[jnp.where(i == 0, 1, 0)]
        k_band = kcur_ref[0, (n - 1) * w:(n + 1) * w, :]
        if half == 1:
            vt_band = vtcur_ref[0, tile]
        else:
            vt_band = jnp.concatenate([vtcur_ref[0, tile - 1, :, w:2 * w], vtcur_ref[0, tile, :, 0:w]], axis=1)
        return k_band, vt_band, bias_ref[0]

    def scores(n, g):
        tile, half = divmod(n, T_DIFF // w)
        k_band, _, bias = band(n)
        qg = qt_ref[0, tile, g * LANES:(g + 1) * LANES, half * w:(half + 1) * w]
        zero = jnp.zeros_like(qg)
        qbd = jnp.concatenate([jnp.where(kv0_feat, qg, zero), jnp.where(kv0_feat, zero, qg)], axis=1)
        s = jnp.dot(k_band, qbd, preferred_element_type=F32)
        s_ref[n * SWA_GROUP + g] = s + bias

    def sink_row(g):
        return jnp.where(first_cols, sink_ref[g], sink_ref[SWA_GROUP + g]) * LOG2E

    def attend(n, g):
        c = n * SWA_GROUP + g
        _, vt_band, _ = band(n)
        v_aug = jnp.concatenate([vt_band, ones_rows], axis=0)
        s = s_ref[c]
        m = jnp.maximum(jnp.max(s, axis=0, keepdims=True), sink_row(g))
        p = jnp.exp2(s - m)
        pv_ref[c] = jnp.dot(v_aug, p.astype(BF16), preferred_element_type=F32)
        m_ref[c] = m

    def finish(n, g):
        c = n * SWA_GROUP + g
        pv = pv_ref[c]
        inv = 1.0 / (pv[LANES:LANES + 1, :] + jnp.exp2(sink_row(g) - m_ref[c]))
        o_pair = jnp.concatenate([pv[0:HEAD_DIM, 0:w] * inv[:, 0:w],
                                  pv[HEAD_DIM:LANES, w:2 * w] * inv[:, w:2 * w]], axis=0)
        o_ref[0, n * w:(n + 1) * w, g * LANES:(g + 1) * LANES] = o_pair.T.astype(BF16)

    order = [(n, g) for n in range(n_sub) for g in range(SWA_GROUP)]
    stages = (scores, attend, finish)
    group = SWA_PIPE_GROUP
    groups = [order[k:k + group] for k in range(0, len(order), group)]
    for tick in range(len(groups) + len(stages) - 1):
        for lag, stage in enumerate(stages):
            if 0 <= tick - lag < len(groups):
                for chain in groups[tick - lag]:
                    stage(*chain)


def _swa_call(proj_a, proj_t, sinks):
    n_q = SEQ // TQ_SWA
    tiles = TQ_SWA // T_DIFF
    subs = TQ_SWA // WINDOW
    return pl.pallas_call(
        _swa_kernel,
        grid=(BATCH, n_q),
        in_specs=[
            pl.BlockSpec(memory_space=pltpu.SMEM),
            pl.BlockSpec((1, tiles, SWA_WIDTH, T_DIFF), lambda b, i: (b, i, T_Q_SWA // SWA_WIDTH, 0)),
            pl.BlockSpec((1, WINDOW, LANES), lambda b, i: (b, jnp.maximum(i * subs - 1, 0), A_K_SWA // LANES)),
            pl.BlockSpec((1, TQ_SWA, LANES), lambda b, i: (b, i, A_K_SWA // LANES)),
            pl.BlockSpec((1, 1, LANES, T_DIFF), lambda b, i: (b, jnp.maximum(i * tiles - 1, 0), T_V_SWA // LANES, 0)),
            pl.BlockSpec((1, tiles, LANES, T_DIFF), lambda b, i: (b, i, T_V_SWA // LANES, 0)),
        ],
        out_specs=pl.BlockSpec((1, TQ_SWA, SWA_WIDTH), lambda b, i: (b, i, 0)),
        out_shape=jax.ShapeDtypeStruct((BATCH, SEQ, SWA_WIDTH), BF16),
        scratch_shapes=[
            pltpu.VMEM((subs * SWA_GROUP, 2 * WINDOW, 2 * WINDOW), F32),
            pltpu.VMEM((subs * SWA_GROUP, LANES + DENOM_ROWS, 2 * WINDOW), F32),
            pltpu.VMEM((subs * SWA_GROUP, 1, 2 * WINDOW), F32),
            pltpu.VMEM((2, 2 * WINDOW, 2 * WINDOW), F32),
        ],
        compiler_params=pltpu.CompilerParams(
            dimension_semantics=("arbitrary", "arbitrary"),
            vmem_limit_bytes=VMEM_LIMIT_BYTES),
        name="swa_attn",
    )(sinks, proj_t, proj_a, proj_a, proj_t, proj_t)


def _post_kernel(x_ref, od_ref, os_ref, wo_ref, gpost1_ref, gt1_ref, gpre2_ref, sc2_ref, sh2_ref,
                 wg_ref, wu_ref, wd_ref, gpost2_ref, gt2_ref, o_ref):
    def mixer_residual(rows):
        x = x_ref[0, rows, :]
        y = jnp.dot(od_ref[0, rows, :], wo_ref[0, 0:DIFF_WIDTH, :], preferred_element_type=F32)
        y = y + jnp.dot(os_ref[0, rows, :], wo_ref[0, DIFF_WIDTH:D_MODEL, :], preferred_element_type=F32)
        x1 = x + gt1_ref[0] * (_rms(y) * gpost1_ref[...])
        h = (_rms(x1) * gpre2_ref[...]) * (1.0 + sc2_ref[0]) + sh2_ref[0]
        return x1, h.astype(BF16)

    def ffn_residual(x1, hb):
        gate = jnp.dot(hb, wg_ref[0], preferred_element_type=F32)
        up = jnp.dot(hb, wu_ref[0], preferred_element_type=F32)
        act = (gate / (1.0 + jnp.exp(-gate))) * up
        y2 = jnp.dot(act.astype(BF16), wd_ref[0], preferred_element_type=F32)
        return x1 + gt2_ref[0] * (_rms(y2) * gpost2_ref[...])

    subs = [slice(k * SUB_POST, (k + 1) * SUB_POST) for k in range(TM_POST // SUB_POST)]
    staged = [mixer_residual(rows) for rows in subs]
    for rows, (x1, hb) in zip(subs, staged):
        o_ref[0, rows, :] = ffn_residual(x1, hb)


def _post_call(layer, x, o_diff, o_swa, mod3, w_out_b, g_post1, g_pre2, w_gate_b, w_up_b, w_down_b, g_post2):
    n_t = SEQ // TM_POST
    row = lambda: pl.BlockSpec((1, D_MODEL), lambda b, i: (0, 0))
    return pl.pallas_call(
        _post_kernel,
        grid=(BATCH, n_t),
        in_specs=[
            pl.BlockSpec((1, TM_POST, D_MODEL), lambda b, i: (b, i, 0)),
            pl.BlockSpec((1, TM_POST, DIFF_WIDTH), lambda b, i: (b, i, 0)),
            pl.BlockSpec((1, TM_POST, SWA_WIDTH), lambda b, i: (b, i, 0)),
            _layer_weight(layer, (D_MODEL, D_MODEL)),
            row(),
            _mod_spec(layer, 2),
            row(),
            _mod_spec(layer, 4),
            _mod_spec(layer, 3),
            _layer_weight(layer, (D_MODEL, D_FF)),
            _layer_weight(layer, (D_MODEL, D_FF)),
            _layer_weight(layer, (D_FF, D_MODEL)),
            row(),
            _mod_spec(layer, 5),
        ],
        out_specs=pl.BlockSpec((1, TM_POST, D_MODEL), lambda b, i: (b, i, 0)),
        out_shape=jax.ShapeDtypeStruct((BATCH, SEQ, D_MODEL), F32),
        compiler_params=pltpu.CompilerParams(
            dimension_semantics=("arbitrary", "arbitrary"),
            vmem_limit_bytes=VMEM_LIMIT_BYTES),
        name="post_ffn",
    )(x, o_diff, o_swa, w_out_b, g_post1.reshape(1, D_MODEL), mod3, g_pre2.reshape(1, D_MODEL),
      mod3, mod3, w_gate_b, w_up_b, w_down_b, g_post2.reshape(1, D_MODEL), mod3)


def kernel(x, c, ada_w, ada_b, g_mix_pre, g_mix_post, g_ffn_pre, g_ffn_post, w_in, lambda_q1, lambda_k1,
           lambda_q2, lambda_k2, subln_g, sinks, w_out, w_gate, w_up, w_down):
    cols_a, cols_t, scale_t = _in_proj_layout()
    out_perm = _out_proj_row_order()
    tables = _rope_tables()

    mod = _ada_call(c, ada_w, ada_b)
    mod3 = mod.reshape(DEPTH * BATCH * 6, 1, D_MODEL)

    w_a = w_in[:, :, cols_a].astype(BF16)
    w_t = jnp.swapaxes(w_in[:, :, cols_t] * scale_t[None, None, :], 1, 2).astype(BF16)
    w_out_b = w_out[:, out_perm, :].astype(BF16)
    w_gate_b, w_up_b, w_down_b = w_gate.astype(BF16), w_up.astype(BF16), w_down.astype(BF16)

    for layer in range(DEPTH):
        lam_vec = jnp.stack([lambda_q1[layer], lambda_k1[layer], lambda_q2[layer], lambda_k2[layer]])
        proj_a, proj_t = _pre_call(layer, x, mod3, g_mix_pre[layer], w_a, w_t, tables)
        o_diff = _diff_call(layer, proj_a, proj_t, lam_vec, subln_g[layer])
        o_swa = _swa_call(proj_a, proj_t, sinks[layer])
        x = _post_call(layer, x, o_diff, o_swa, mod3, w_out_b, g_mix_post[layer], g_ffn_pre[layer],
                       w_gate_b, w_up_b, w_down_b, g_ffn_post[layer])
    return x
```

```python
import functools
import math

import numpy as np
import jax
import jax.numpy as jnp
from jax import lax
from jax.experimental import pallas as pl
from jax.experimental.pallas import tpu as pltpu

D_MODEL = 1024
BATCH = 16
SEQ = 2048
DEPTH = 2
HEAD_DIM = 64
HALF = HEAD_DIM // 2
DIFF_HEADS = 4
DIFF_WIDTH = 512
SWA_HEADS = 8
SWA_KV_HEADS = 2
SWA_GROUP = SWA_HEADS // SWA_KV_HEADS
SWA_WIDTH = 512
WINDOW = 128
ROPE_THETA = 10000.0
D_FF = 2816
EPS = 1e-6
NEG = -1e30
LOG2E = math.log2(math.e)

LANES = 128

REF_Q_DIFF, REF_K_DIFF, REF_V_DIFF, REF_Q_SWA, REF_K_SWA, REF_V_SWA = 0, 512, 1024, 1536, 2048, 2176
A_K_DIFF, A_K_SWA, A_WIDTH = 0, 512, 640
T_Q_DIFF, T_V_DIFF, T_Q_SWA, T_V_SWA, T_WIDTH = 0, 512, 1024, 1536, 1664
T_ROPE_ROWS = ((T_Q_DIFF, DIFF_WIDTH), (T_Q_SWA, SWA_WIDTH))

VMEM_LIMIT_BYTES = 56 * 1024 * 1024

TM_PRE = 512
TM_POST = 1024
SUB_POST = 256
T_DIFF = 256
DENOM_ROWS = 16
TQ_SWA = 512
SWA_PIPE_GROUP = 4

F32 = jnp.float32
BF16 = jnp.bfloat16
NT_DIMS = (((1,), (1,)), ((), ()))


def _projection_weights(w_in):
    def section(start, width):
        return w_in[:, :, start:start + width]

    def halves_outer(w, pairs):
        w = w.reshape(DEPTH, D_MODEL, pairs, 2, 2, HALF)
        return jnp.swapaxes(w, 3, 4).reshape(DEPTH, D_MODEL, pairs * LANES)

    score_scale = HEAD_DIM ** -0.5
    q_diff = halves_outer(section(REF_Q_DIFF, DIFF_WIDTH), DIFF_HEADS) * score_scale
    k_diff = halves_outer(section(REF_K_DIFF, DIFF_WIDTH), DIFF_HEADS)
    v_diff = section(REF_V_DIFF, DIFF_WIDTH)
    q_swa = section(REF_Q_SWA, SWA_WIDTH).reshape(DEPTH, D_MODEL, SWA_KV_HEADS, SWA_GROUP, 2, HALF)
    q_swa = q_swa.transpose(0, 1, 3, 4, 2, 5).reshape(DEPTH, D_MODEL, SWA_WIDTH) * score_scale
    k_swa = halves_outer(section(REF_K_SWA, LANES), 1)
    v_swa = section(REF_V_SWA, LANES)
    w_a = jnp.concatenate([k_diff, k_swa], axis=-1).astype(BF16)
    w_t = jnp.swapaxes(jnp.concatenate([q_diff, v_diff, q_swa, v_swa], axis=-1), 1, 2).astype(BF16)
    return w_a, w_t


def _out_proj_weight(w_out):
    diff_rows = w_out[:, 0:DIFF_WIDTH, :]
    swa_rows = w_out[:, DIFF_WIDTH:, :].reshape(DEPTH, SWA_KV_HEADS, SWA_GROUP, HEAD_DIM, D_MODEL)
    swa_rows = jnp.swapaxes(swa_rows, 1, 2).reshape(DEPTH, SWA_WIDTH, D_MODEL)
    return jnp.concatenate([diff_rows, swa_rows], axis=1).astype(BF16)


def _rope_tables():
    pos = jnp.arange(SEQ, dtype=F32)
    inv = ROPE_THETA ** (-jnp.arange(0, HEAD_DIM, 2, dtype=F32) / HEAD_DIM)
    ang = pos[:, None] * inv[None, :]
    cos, sin = jnp.cos(ang), jnp.sin(ang)
    cos4 = jnp.concatenate([cos, cos, cos, cos], axis=-1)
    sin4 = jnp.concatenate([-sin, -sin, sin, sin], axis=-1)
    cos_t = jnp.concatenate([cos, cos], axis=-1).T * LOG2E
    sin_t = jnp.concatenate([sin, sin], axis=-1).T * LOG2E
    return cos4, sin4, cos_t, sin_t


def _rms(x):
    return x * lax.rsqrt(jnp.mean(x * x, axis=-1, keepdims=True) + EPS)


def _ada_kernel(c_ref, w_ref, b_ref, o_ref):
    c = c_ref[...]
    c_act = c / (1.0 + jnp.exp(-c))
    o_ref[0] = jnp.dot(c_act, w_ref[0], preferred_element_type=F32) + b_ref[0]


def _ada_call(c, ada_w, ada_b):
    n_chunk = 6
    return pl.pallas_call(
        _ada_kernel,
        grid=(DEPTH, n_chunk),
        in_specs=[
            pl.BlockSpec((BATCH, D_MODEL), lambda l, k: (0, 0)),
            pl.BlockSpec((1, D_MODEL, D_MODEL), lambda l, k: (l, 0, k)),
            pl.BlockSpec((1, 1, D_MODEL), lambda l, k: (l, 0, k)),
        ],
        out_specs=pl.BlockSpec((1, BATCH, D_MODEL), lambda l, k: (l, 0, k)),
        out_shape=jax.ShapeDtypeStruct((DEPTH, BATCH, 6 * D_MODEL), F32),
        name="ada_mod",
    )(c, ada_w, ada_b.reshape(DEPTH, 1, 6 * D_MODEL))


def _pre_kernel(x_ref, g_ref, sh_ref, sc_ref, wa_ref, wt_ref, cos_ref, sin_ref, cost_ref, sint_ref,
                oa_ref, ot_ref):
    def modulate(tok):
        x = x_ref[0, tok, :]
        h = (_rms(x) * g_ref[...]) * (1.0 + sc_ref[0]) + sh_ref[0]
        return h.astype(BF16)

    def project(blk, tok, hb):
        proj = jnp.dot(hb, wa_ref[0], preferred_element_type=F32)
        cos = cos_ref[tok, :]
        sin = sin_ref[tok, :]
        for grp in range(A_WIDTH // LANES):
            cols = slice(grp * LANES, (grp + 1) * LANES)
            xg = proj[:, cols]
            oa_ref[0, tok, cols] = (xg * cos + pltpu.roll(xg, 64, 1) * sin).astype(BF16)

        proj_t = lax.dot_general(wt_ref[0], hb, NT_DIMS, preferred_element_type=F32)
        cos_t = cost_ref[:, tok]
        sin_t = sint_ref[:, tok]
        for start, size in T_ROPE_ROWS:
            for grp in range(size // LANES):
                base = start + grp * LANES
                lo = proj_t[base:base + 64]
                hi = proj_t[base + 64:base + LANES]
                ot_ref[0, blk, base:base + 64, :] = (lo * cos_t - hi * sin_t).astype(BF16)
                ot_ref[0, blk, base + 64:base + LANES, :] = (hi * cos_t + lo * sin_t).astype(BF16)
        for start, size in ((T_V_DIFF, DIFF_WIDTH), (T_V_SWA, LANES)):
            ot_ref[0, blk, start:start + size, :] = proj_t[start:start + size].astype(BF16)

    toks = [slice(blk * T_DIFF, (blk + 1) * T_DIFF) for blk in range(TM_PRE // T_DIFF)]
    staged = [modulate(tok) for tok in toks]
    for blk, (tok, hb) in enumerate(zip(toks, staged)):
        project(blk, tok, hb)


def _mod_spec(layer, which):
    def index(b, i):
        return ((layer * BATCH + b) * 6 + which, 0, 0)
    return pl.BlockSpec((1, 1, D_MODEL), index)


def _layer_weight(layer, shape):
    return pl.BlockSpec((1,) + shape, lambda b, i: (layer,) + (0,) * len(shape), pipeline_mode=pl.Buffered(1))


def _pre_call(layer, x, mod3, g_pre, w_a, w_t, tables):
    cos4, sin4, cos_t, sin_t = tables
    n_t = SEQ // TM_PRE
    n_blk = TM_PRE // T_DIFF
    return pl.pallas_call(
        _pre_kernel,
        grid=(BATCH, n_t),
        in_specs=[
            pl.BlockSpec((1, TM_PRE, D_MODEL), lambda b, i: (b, i, 0)),
            pl.BlockSpec((1, D_MODEL), lambda b, i: (0, 0)),
            _mod_spec(layer, 0),
            _mod_spec(layer, 1),
            _layer_weight(layer, (D_MODEL, A_WIDTH)),
            _layer_weight(layer, (T_WIDTH, D_MODEL)),
            pl.BlockSpec((TM_PRE, LANES), lambda b, i: (i, 0)),
            pl.BlockSpec((TM_PRE, LANES), lambda b, i: (i, 0)),
            pl.BlockSpec((HEAD_DIM, TM_PRE), lambda b, i: (0, i)),
            pl.BlockSpec((HEAD_DIM, TM_PRE), lambda b, i: (0, i)),
        ],
        out_specs=[
            pl.BlockSpec((1, TM_PRE, A_WIDTH), lambda b, i: (b, i, 0)),
            pl.BlockSpec((1, n_blk, T_WIDTH, T_DIFF), lambda b, i: (b, i, 0, 0)),
        ],
        out_shape=[
            jax.ShapeDtypeStruct((BATCH, SEQ, A_WIDTH), BF16),
            jax.ShapeDtypeStruct((BATCH, SEQ // T_DIFF, T_WIDTH, T_DIFF), BF16),
        ],
        compiler_params=pltpu.CompilerParams(
            dimension_semantics=("arbitrary", "arbitrary"),
            vmem_limit_bytes=VMEM_LIMIT_BYTES),
        name="pre_mixer",
    )(x, g_pre.reshape(1, D_MODEL), mod3, mod3, w_a, w_t, cos4, sin4, cos_t, sin_t)


def _diff_kernel(lam_ref, g_ref, qt_ref, k_ref, vt_ref, o_ref,
                 qbd_ref, m_ref, acc_ref, s0_ref, s1_ref, max0_ref, max1_ref, *, lam_init):
    pair_idx = pl.program_id(1)
    t = T_DIFF
    feat = lax.broadcasted_iota(jnp.int32, (LANES, 1), 0)
    first_map = ((feat // HALF) % 2) == 0

    chains = [(sub, h) for sub in range(2) for h in range(DIFF_HEADS)]
    sub_chains = lambda sub: [c for c, (sb, _) in enumerate(chains) if sb == sub]
    all_chains = list(range(len(chains)))

    for c, (sub, h) in enumerate(chains):
        qh = qt_ref[0, sub, h * LANES:(h + 1) * LANES, :]
        zero = jnp.zeros_like(qh)
        qbd_ref[c, :, 0:t] = jnp.where(first_map, qh, zero)
        qbd_ref[c, :, t:2 * t] = jnp.where(first_map, zero, qh)
        m_ref[c] = jnp.full((1, 2 * t), NEG, F32)
        acc_ref[c] = jnp.zeros((LANES + DENOM_ROWS, 2 * t), F32)
    ones_rows = jnp.ones((DENOM_ROWS, t), BF16)

    def scores(j, c, s_ref, max_ref):
        h = chains[c][1]
        start = pl.multiple_of(j * t, t)
        kh = k_ref[0, pl.ds(start, t), h * LANES:(h + 1) * LANES]
        s = jnp.dot(kh, qbd_ref[c], preferred_element_type=F32)
        s_ref[c] = s
        max_ref[c] = jnp.max(s, axis=0, keepdims=True)

    def accumulate(j, c, s_ref, max_ref, diagonal):
        h = chains[c][1]
        vth = vt_ref[0, j, h * LANES:(h + 1) * LANES, :]
        s = s_ref[c]
        if diagonal:
            key = lax.broadcasted_iota(jnp.int32, (t, 2 * t), 0)
            qry = lax.broadcasted_iota(jnp.int32, (t, 2 * t), 1) & (t - 1)
            s = jnp.where(key <= qry, s, NEG)
            m_blk = jnp.max(s, axis=0, keepdims=True)
        else:
            m_blk = max_ref[c]
        m_old = m_ref[c]
        m_new = jnp.maximum(m_old, m_blk)
        p = jnp.exp2(s - m_new)
        alpha = jnp.exp2(m_old - m_new)
        v_aug = jnp.concatenate([vth, ones_rows], axis=0)
        acc_ref[c] = alpha * acc_ref[c] + jnp.dot(v_aug, p.astype(BF16), preferred_element_type=F32)
        m_ref[c] = m_new

    for c in all_chains:
        scores(0, c, s0_ref, max0_ref)

    def two_blocks(step, carry):
        j = 2 * step
        for c in all_chains:
            scores(j + 1, c, s1_ref, max1_ref)
            accumulate(j, c, s0_ref, max0_ref, False)
        for c in all_chains:
            scores(j + 2, c, s0_ref, max0_ref)
            accumulate(j + 1, c, s1_ref, max1_ref, False)
        return carry

    lax.fori_loop(0, pair_idx, two_blocks, 0)

    lam_vec = lam_ref[...]
    dot1 = jnp.sum(lam_vec[0:1] * lam_vec[1:2], axis=-1, keepdims=True)
    dot2 = jnp.sum(lam_vec[2:3] * lam_vec[3:4], axis=-1, keepdims=True)
    lam = jnp.exp(dot1) - jnp.exp(dot2) + lam_init
    g = g_ref[...]

    def finalize(c):
        sub, h = chains[c]
        denom = acc_ref[c, LANES:LANES + 1, :]
        o_all = acc_ref[c, 0:LANES, :] * (1.0 / denom)
        o = o_all[:, 0:t] - lam * o_all[:, t:2 * t]
        o = o * lax.rsqrt(jnp.mean(o * o, axis=0, keepdims=True) + EPS)
        o = (o * g) * (1.0 - lam_init)
        o_ref[0, sub * t:(sub + 1) * t, h * LANES:(h + 1) * LANES] = o.T.astype(BF16)

    diag0 = 2 * pair_idx
    for c0, c1 in zip(sub_chains(0), sub_chains(1)):
        scores(diag0 + 1, c1, s1_ref, max1_ref)
        accumulate(diag0, c0, s0_ref, max0_ref, True)
        accumulate(diag0, c1, s0_ref, max0_ref, False)
    for c0, c1 in zip(sub_chains(0), sub_chains(1)):
        accumulate(diag0 + 1, c1, s1_ref, max1_ref, True)
        finalize(c0)
    for c1 in sub_chains(1):
        finalize(c1)


def _diff_call(layer, proj_a, proj_t, lam_vec, subln_g):
    lam_init = 0.8 - 0.6 * math.exp(-0.3 * layer)
    n_blocks = SEQ // T_DIFF
    n_chains = 2 * DIFF_HEADS
    kernel = functools.partial(_diff_kernel, lam_init=lam_init)
    return pl.pallas_call(
        kernel,
        grid=(BATCH, n_blocks // 2),
        in_specs=[
            pl.BlockSpec((4, HEAD_DIM), lambda b, i: (0, 0)),
            pl.BlockSpec((LANES, 1), lambda b, i: (0, 0)),
            pl.BlockSpec((1, 2, DIFF_WIDTH, T_DIFF), lambda b, i: (b, i, T_Q_DIFF // DIFF_WIDTH, 0)),
            pl.BlockSpec((1, SEQ, DIFF_WIDTH), lambda b, i: (b, 0, A_K_DIFF // DIFF_WIDTH)),
            pl.BlockSpec((1, n_blocks, DIFF_WIDTH, T_DIFF), lambda b, i: (b, 0, T_V_DIFF // DIFF_WIDTH, 0)),
        ],
        out_specs=pl.BlockSpec((1, 2 * T_DIFF, DIFF_WIDTH), lambda b, i: (b, i, 0)),
        out_shape=jax.ShapeDtypeStruct((BATCH, SEQ, DIFF_WIDTH), BF16),
        scratch_shapes=[
            pltpu.VMEM((n_chains, LANES, 2 * T_DIFF), BF16),
            pltpu.VMEM((n_chains, 1, 2 * T_DIFF), F32),
            pltpu.VMEM((n_chains, LANES + DENOM_ROWS, 2 * T_DIFF), F32),
            pltpu.VMEM((n_chains, T_DIFF, 2 * T_DIFF), F32),
            pltpu.VMEM((n_chains, T_DIFF, 2 * T_DIFF), F32),
            pltpu.VMEM((n_chains, 1, 2 * T_DIFF), F32),
            pltpu.VMEM((n_chains, 1, 2 * T_DIFF), F32),
        ],
        compiler_params=pltpu.CompilerParams(
            dimension_semantics=("arbitrary", "arbitrary"),
            vmem_limit_bytes=VMEM_LIMIT_BYTES),
        name="diff_attn",
    )(lam_vec, subln_g.reshape(LANES, 1), proj_t, proj_a, proj_t)


def _swa_kernel(sink_ref, qt_ref, kprev_ref, kcur_ref, vtprev_ref, vtcur_ref, o_ref,
                s_ref, pv_ref, m_ref, bias_ref):
    i = pl.program_id(1)
    w = WINDOW
    n_sub = TQ_SWA // w
    feat = lax.broadcasted_iota(jnp.int32, (LANES, 1), 0)
    kv0_feat = ((feat // HALF) % 2) == 0
    col = lax.broadcasted_iota(jnp.int32, (1, 2 * w), 1)
    first_cols = col < w

    @pl.when(i == 0)
    def _():
        key = lax.broadcasted_iota(jnp.int32, (2 * w, 2 * w), 0)
        qry = lax.broadcasted_iota(jnp.int32, (2 * w, 2 * w), 1) & (w - 1)
        in_window = (key > qry) & (key <= qry + w)
        bias_ref[0] = jnp.where(in_window, 0.0, NEG)
        bias_ref[1] = jnp.where(in_window & (key >= w), 0.0, NEG)
    ones_rows = jnp.ones((DENOM_ROWS, 2 * w), BF16)

    def band(n):
        tile, half = divmod(n, T_DIFF // w)
        if n == 0:
            k_band = jnp.concatenate([kprev_ref[0], kcur_ref[0, 0:w, :]], axis=0)
            vt_band = jnp.concatenate([vtprev_ref[0, 0, :, w:2 * w], vtcur_ref[0, 0, :, 0:w]], axis=1)
            return k_band, vt_band, bias_ref[jnp.where(i == 0, 1, 0)]
        k_band = kcur_ref[0, (n - 1) * w:(n + 1) * w, :]
        if half == 1:
            vt_band = vtcur_ref[0, tile]
        else:
            vt_band = jnp.concatenate([vtcur_ref[0, tile - 1, :, w:2 * w], vtcur_ref[0, tile, :, 0:w]], axis=1)
        return k_band, vt_band, bias_ref[0]

    def scores(n, g):
        tile, half = divmod(n, T_DIFF // w)
        k_band, _, bias = band(n)
        qg = qt_ref[0, tile, g * LANES:(g + 1) * LANES, half * w:(half + 1) * w]
        zero = jnp.zeros_like(qg)
        qbd = jnp.concatenate([jnp.where(kv0_feat, qg, zero), jnp.where(kv0_feat, zero, qg)], axis=1)
        s = jnp.dot(k_band, qbd, preferred_element_type=F32)
        s_ref[n * SWA_GROUP + g] = s + bias

    def sink_row(g):
        return jnp.where(first_cols, sink_ref[g], sink_ref[SWA_GROUP + g]) * LOG2E

    def attend(n, g):
        c = n * SWA_GROUP + g
        _, vt_band, _ = band(n)
        v_aug = jnp.concatenate([vt_band, ones_rows], axis=0)
        s = s_ref[c]
        m = jnp.maximum(jnp.max(s, axis=0, keepdims=True), sink_row(g))
        p = jnp.exp2(s - m)
        pv_ref[c] = jnp.dot(v_aug, p.astype(BF16), preferred_element_type=F32)
        m_ref[c] = m

    def finish(n, g):
        c = n * SWA_GROUP + g
        pv = pv_ref[c]
        inv = 1.0 / (pv[LANES:LANES + 1, :] + jnp.exp2(sink_row(g) - m_ref[c]))
        o_pair = jnp.concatenate([pv[0:HEAD_DIM, 0:w] * inv[:, 0:w],
                                  pv[HEAD_DIM:LANES, w:2 * w] * inv[:, w:2 * w]], axis=0)
        o_ref[0, n * w:(n + 1) * w, g * LANES:(g + 1) * LANES] = o_pair.T.astype(BF16)

    order = [(n, g) for n in range(n_sub) for g in range(SWA_GROUP)]
    stages = (scores, attend, finish)
    group = SWA_PIPE_GROUP
    groups = [order[k:k + group] for k in range(0, len(order), group)]
    for tick in range(len(groups) + len(stages) - 1):
        for lag, stage in enumerate(stages):
            if 0 <= tick - lag < len(groups):
                for chain in groups[tick - lag]:
                    stage(*chain)


def _swa_call(proj_a, proj_t, sinks):
    n_q = SEQ // TQ_SWA
    tiles = TQ_SWA // T_DIFF
    subs = TQ_SWA // WINDOW
    return pl.pallas_call(
        _swa_kernel,
        grid=(BATCH, n_q),
        in_specs=[
            pl.BlockSpec(memory_space=pltpu.SMEM),
            pl.BlockSpec((1, tiles, SWA_WIDTH, T_DIFF), lambda b, i: (b, i, T_Q_SWA // SWA_WIDTH, 0)),
            pl.BlockSpec((1, WINDOW, LANES), lambda b, i: (b, jnp.maximum(i * subs - 1, 0), A_K_SWA // LANES)),
            pl.BlockSpec((1, TQ_SWA, LANES), lambda b, i: (b, i, A_K_SWA // LANES)),
            pl.BlockSpec((1, 1, LANES, T_DIFF), lambda b, i: (b, jnp.maximum(i * tiles - 1, 0), T_V_SWA // LANES, 0)),
            pl.BlockSpec((1, tiles, LANES, T_DIFF), lambda b, i: (b, i, T_V_SWA // LANES, 0)),
        ],
        out_specs=pl.BlockSpec((1, TQ_SWA, SWA_WIDTH), lambda b, i: (b, i, 0)),
        out_shape=jax.ShapeDtypeStruct((BATCH, SEQ, SWA_WIDTH), BF16),
        scratch_shapes=[
            pltpu.VMEM((subs * SWA_GROUP, 2 * WINDOW, 2 * WINDOW), F32),
            pltpu.VMEM((subs * SWA_GROUP, LANES + DENOM_ROWS, 2 * WINDOW), F32),
            pltpu.VMEM((subs * SWA_GROUP, 1, 2 * WINDOW), F32),
            pltpu.VMEM((2, 2 * WINDOW, 2 * WINDOW), F32),
        ],
        compiler_params=pltpu.CompilerParams(
            dimension_semantics=("arbitrary", "arbitrary"),
            vmem_limit_bytes=VMEM_LIMIT_BYTES),
        name="swa_attn",
    )(sinks, proj_t, proj_a, proj_a, proj_t, proj_t)


def _post_kernel(x_ref, od_ref, os_ref, wo_ref, gpost1_ref, gt1_ref, gpre2_ref, sc2_ref, sh2_ref,
                 wg_ref, wu_ref, wd_ref, gpost2_ref, gt2_ref, o_ref):
    def mixer_residual(rows):
        x = x_ref[0, rows, :]
        y = jnp.dot(od_ref[0, rows, :], wo_ref[0, 0:DIFF_WIDTH, :], preferred_element_type=F32)
        y = y + jnp.dot(os_ref[0, rows, :], wo_ref[0, DIFF_WIDTH:D_MODEL, :], preferred_element_type=F32)
        x1 = x + gt1_ref[0] * (_rms(y) * gpost1_ref[...])
        h = (_rms(x1) * gpre2_ref[...]) * (1.0 + sc2_ref[0]) + sh2_ref[0]
        return x1, h.astype(BF16)

    def ffn_residual(x1, hb):
        gate = jnp.dot(hb, wg_ref[0], preferred_element_type=F32)
        up = jnp.dot(hb, wu_ref[0], preferred_element_type=F32)
        act = (gate / (1.0 + jnp.exp(-gate))) * up
        y2 = jnp.dot(act.astype(BF16), wd_ref[0], preferred_element_type=F32)
        return x1 + gt2_ref[0] * (_rms(y2) * gpost2_ref[...])

    subs = [slice(k * SUB_POST, (k + 1) * SUB_POST) for k in range(TM_POST // SUB_POST)]
    staged = [mixer_residual(rows) for rows in subs]
    for rows, (x1, hb) in zip(subs, staged):
        o_ref[0, rows, :] = ffn_residual(x1, hb)


def _post_call(layer, x, o_diff, o_swa, mod3, w_out_b, g_post1, g_pre2, w_gate_b, w_up_b, w_down_b, g_post2):
    n_t = SEQ // TM_POST
    row = lambda: pl.BlockSpec((1, D_MODEL), lambda b, i: (0, 0))
    return pl.pallas_call(
        _post_kernel,
        grid=(BATCH, n_t),
        in_specs=[
            pl.BlockSpec((1, TM_POST, D_MODEL), lambda b, i: (b, i, 0)),
            pl.BlockSpec((1, TM_POST, DIFF_WIDTH), lambda b, i: (b, i, 0)),
            pl.BlockSpec((1, TM_POST, SWA_WIDTH), lambda b, i: (b, i, 0)),
            _layer_weight(layer, (D_MODEL, D_MODEL)),
            row(),
            _mod_spec(layer, 2),
            row(),
            _mod_spec(layer, 4),
            _mod_spec(layer, 3),
            _layer_weight(layer, (D_MODEL, D_FF)),
            _layer_weight(layer, (D_MODEL, D_FF)),
            _layer_weight(layer, (D_FF, D_MODEL)),
            row(),
            _mod_spec(layer, 5),
        ],
        out_specs=pl.BlockSpec((1, TM_POST, D_MODEL), lambda b, i: (b, i, 0)),
        out_shape=jax.ShapeDtypeStruct((BATCH, SEQ, D_MODEL), F32),
        compiler_params=pltpu.CompilerParams(
            dimension_semantics=("arbitrary", "arbitrary"),
            vmem_limit_bytes=VMEM_LIMIT_BYTES),
        name="post_ffn",
    )(x, o_diff, o_swa, w_out_b, g_post1.reshape(1, D_MODEL), mod3, g_pre2.reshape(1, D_MODEL),
      mod3, mod3, w_gate_b, w_up_b, w_down_b, g_post2.reshape(1, D_MODEL), mod3)


def kernel(x, c, ada_w, ada_b, g_mix_pre, g_mix_post, g_ffn_pre, g_ffn_post, w_in, lambda_q1, lambda_k1,
           lambda_q2, lambda_k2, subln_g, sinks, w_out, w_gate, w_up, w_down):
    tables = _rope_tables()

    mod = _ada_call(c, ada_w, ada_b)
    mod3 = mod.reshape(DEPTH * BATCH * 6, 1, D_MODEL)

    w_a, w_t = _projection_weights(w_in)
    w_out_b = _out_proj_weight(w_out)
    w_gate_b, w_up_b, w_down_b = w_gate.astype(BF16), w_up.astype(BF16), w_down.astype(BF16)

    for layer in range(DEPTH):
        lam_vec = jnp.stack([lambda_q1[layer], lambda_k1[layer], lambda_q2[layer], lambda_k2[layer]])
        proj_a, proj_t = _pre_call(layer, x, mod3, g_mix_pre[layer], w_a, w_t, tables)
        o_diff = _diff_call(layer, proj_a, proj_t, lam_vec, subln_g[layer])
        o_swa = _swa_call(proj_a, proj_t, sinks[layer])
        x = _post_call(layer, x, o_diff, o_swa, mod3, w_out_b, g_mix_post[layer], g_ffn_pre[layer],
                       w_gate_b, w_up_b, w_down_b, g_ffn_post[layer])
    return x
```

```python
import functools
import math

import numpy as np
import jax
import jax.numpy as jnp
from jax import lax
from jax.experimental import pallas as pl
from jax.experimental.pallas import tpu as pltpu

D_MODEL = 1024
BATCH = 16
SEQ = 2048
DEPTH = 2
HEAD_DIM = 64
HALF = HEAD_DIM // 2
DIFF_HEADS = 4
DIFF_WIDTH = 512
SWA_HEADS = 8
SWA_KV_HEADS = 2
SWA_GROUP = SWA_HEADS // SWA_KV_HEADS
SWA_WIDTH = 512
WINDOW = 128
ROPE_THETA = 10000.0
D_FF = 2816
EPS = 1e-6
NEG = -1e30
LOG2E = math.log2(math.e)

LANES = 128

REF_Q_DIFF, REF_K_DIFF, REF_V_DIFF, REF_Q_SWA, REF_K_SWA, REF_V_SWA = 0, 512, 1024, 1536, 2048, 2176
A_K_DIFF, A_K_SWA, A_WIDTH = 0, 512, 640
T_Q_DIFF, T_V_DIFF, T_Q_SWA, T_V_SWA, T_WIDTH = 0, 512, 1024, 1536, 1664
T_ROPE_ROWS = ((T_Q_DIFF, DIFF_WIDTH), (T_Q_SWA, SWA_WIDTH))

VMEM_LIMIT_BYTES = 56 * 1024 * 1024

TM_PRE = 512
TM_POST = 1024
SUB_POST = 256
T_DIFF = 256
DENOM_ROWS = 16
TQ_SWA = 2048
SWA_PIPE_GROUP = 4

F32 = jnp.float32
BF16 = jnp.bfloat16
NT_DIMS = (((1,), (1,)), ((), ()))


def _projection_weights(w_in):
    def section(start, width):
        return w_in[:, :, start:start + width]

    def halves_outer(w, pairs):
        w = w.reshape(DEPTH, D_MODEL, pairs, 2, 2, HALF)
        return jnp.swapaxes(w, 3, 4).reshape(DEPTH, D_MODEL, pairs * LANES)

    score_scale = HEAD_DIM ** -0.5
    q_diff = halves_outer(section(REF_Q_DIFF, DIFF_WIDTH), DIFF_HEADS) * score_scale
    k_diff = halves_outer(section(REF_K_DIFF, DIFF_WIDTH), DIFF_HEADS)
    v_diff = section(REF_V_DIFF, DIFF_WIDTH)
    q_swa = section(REF_Q_SWA, SWA_WIDTH).reshape(DEPTH, D_MODEL, SWA_KV_HEADS, SWA_GROUP, 2, HALF)
    q_swa = q_swa.transpose(0, 1, 3, 4, 2, 5).reshape(DEPTH, D_MODEL, SWA_WIDTH) * score_scale
    k_swa = halves_outer(section(REF_K_SWA, LANES), 1)
    v_swa = section(REF_V_SWA, LANES)
    w_a = jnp.concatenate([k_diff, k_swa], axis=-1).astype(BF16)
    w_t = jnp.swapaxes(jnp.concatenate([q_diff, v_diff, q_swa, v_swa], axis=-1), 1, 2).astype(BF16)
    return w_a, w_t


def _out_proj_weight(w_out):
    diff_rows = w_out[:, 0:DIFF_WIDTH, :]
    swa_rows = w_out[:, DIFF_WIDTH:, :].reshape(DEPTH, SWA_KV_HEADS, SWA_GROUP, HEAD_DIM, D_MODEL)
    swa_rows = jnp.swapaxes(swa_rows, 1, 2).reshape(DEPTH, SWA_WIDTH, D_MODEL)
    return jnp.concatenate([diff_rows, swa_rows], axis=1).astype(BF16)


def _rope_tables():
    pos = jnp.arange(SEQ, dtype=F32)
    inv = ROPE_THETA ** (-jnp.arange(0, HEAD_DIM, 2, dtype=F32) / HEAD_DIM)
    ang = pos[:, None] * inv[None, :]
    cos, sin = jnp.cos(ang), jnp.sin(ang)
    cos4 = jnp.concatenate([cos, cos, cos, cos], axis=-1)
    sin4 = jnp.concatenate([-sin, -sin, sin, sin], axis=-1)
    cos_t = jnp.concatenate([cos, cos], axis=-1).T * LOG2E
    sin_t = jnp.concatenate([sin, sin], axis=-1).T * LOG2E
    return cos4, sin4, cos_t, sin_t


def _rms(x):
    return x * lax.rsqrt(jnp.mean(x * x, axis=-1, keepdims=True) + EPS)


def _ada_kernel(c_ref, w_ref, b_ref, o_ref):
    c = c_ref[...]
    c_act = c / (1.0 + jnp.exp(-c))
    o_ref[0] = jnp.dot(c_act, w_ref[0], preferred_element_type=F32) + b_ref[0]


def _ada_call(c, ada_w, ada_b):
    n_chunk = 6
    return pl.pallas_call(
        _ada_kernel,
        grid=(DEPTH, n_chunk),
        in_specs=[
            pl.BlockSpec((BATCH, D_MODEL), lambda l, k: (0, 0)),
            pl.BlockSpec((1, D_MODEL, D_MODEL), lambda l, k: (l, 0, k)),
            pl.BlockSpec((1, 1, D_MODEL), lambda l, k: (l, 0, k)),
        ],
        out_specs=pl.BlockSpec((1, BATCH, D_MODEL), lambda l, k: (l, 0, k)),
        out_shape=jax.ShapeDtypeStruct((DEPTH, BATCH, 6 * D_MODEL), F32),
        name="ada_mod",
    )(c, ada_w, ada_b.reshape(DEPTH, 1, 6 * D_MODEL))


def _pre_kernel(x_ref, gain_ref, mod_ref, wa_ref, wt_ref, cos_ref, sin_ref, cost_ref, sint_ref,
                oa_ref, ot_ref):
    gain = _row(gain_ref, GAIN_MIX_PRE)
    shift, scale = _row(mod_ref, MOD_SH1), _row(mod_ref, MOD_SC1)

    def modulate(tok):
        x = x_ref[0, tok, :]
        h = (_rms(x) * gain) * (1.0 + scale) + shift
        return h.astype(BF16)

    def project(blk, tok, hb):
        proj = jnp.dot(hb, wa_ref[0], preferred_element_type=F32)
        cos = cos_ref[tok, :]
        sin = sin_ref[tok, :]
        for grp in range(A_WIDTH // LANES):
            cols = slice(grp * LANES, (grp + 1) * LANES)
            xg = proj[:, cols]
            oa_ref[0, tok, cols] = (xg * cos + pltpu.roll(xg, 64, 1) * sin).astype(BF16)

        proj_t = lax.dot_general(wt_ref[0], hb, NT_DIMS, preferred_element_type=F32)
        cos_t = cost_ref[:, tok]
        sin_t = sint_ref[:, tok]
        for start, size in T_ROPE_ROWS:
            for grp in range(size // LANES):
                base = start + grp * LANES
                lo = proj_t[base:base + 64]
                hi = proj_t[base + 64:base + LANES]
                ot_ref[0, blk, base:base + 64, :] = (lo * cos_t - hi * sin_t).astype(BF16)
                ot_ref[0, blk, base + 64:base + LANES, :] = (hi * cos_t + lo * sin_t).astype(BF16)
        for start, size in ((T_V_DIFF, DIFF_WIDTH), (T_V_SWA, LANES)):
            ot_ref[0, blk, start:start + size, :] = proj_t[start:start + size].astype(BF16)

    toks = [slice(blk * T_DIFF, (blk + 1) * T_DIFF) for blk in range(TM_PRE // T_DIFF)]
    staged = [modulate(tok) for tok in toks]
    for blk, (tok, hb) in enumerate(zip(toks, staged)):
        project(blk, tok, hb)


MOD_SH1, MOD_SC1, MOD_GT1, MOD_SH2, MOD_SC2, MOD_GT2 = range(6)
GAIN_MIX_PRE, GAIN_MIX_POST, GAIN_FFN_PRE, GAIN_FFN_POST = range(4)


def _mod_spec(layer):
    return pl.BlockSpec((1, 6, D_MODEL), lambda b, i: (layer * BATCH + b, 0, 0))


def _gain_spec(layer):
    return pl.BlockSpec((1, 4, D_MODEL), lambda b, i: (layer, 0, 0))


def _row(ref, k):
    return ref[0, k:k + 1, :]


def _layer_weight(layer, shape):
    return pl.BlockSpec((1,) + shape, lambda b, i: (layer,) + (0,) * len(shape), pipeline_mode=pl.Buffered(1))


def _pre_call(layer, x, mod3, gains, w_a, w_t, tables):
    cos4, sin4, cos_t, sin_t = tables
    n_t = SEQ // TM_PRE
    n_blk = TM_PRE // T_DIFF
    return pl.pallas_call(
        _pre_kernel,
        grid=(BATCH, n_t),
        in_specs=[
            pl.BlockSpec((1, TM_PRE, D_MODEL), lambda b, i: (b, i, 0)),
            _gain_spec(layer),
            _mod_spec(layer),
            _layer_weight(layer, (D_MODEL, A_WIDTH)),
            _layer_weight(layer, (T_WIDTH, D_MODEL)),
            pl.BlockSpec((TM_PRE, LANES), lambda b, i: (i, 0)),
            pl.BlockSpec((TM_PRE, LANES), lambda b, i: (i, 0)),
            pl.BlockSpec((HEAD_DIM, TM_PRE), lambda b, i: (0, i)),
            pl.BlockSpec((HEAD_DIM, TM_PRE), lambda b, i: (0, i)),
        ],
        out_specs=[
            pl.BlockSpec((1, TM_PRE, A_WIDTH), lambda b, i: (b, i, 0)),
            pl.BlockSpec((1, n_blk, T_WIDTH, T_DIFF), lambda b, i: (b, i, 0, 0)),
        ],
        out_shape=[
            jax.ShapeDtypeStruct((BATCH, SEQ, A_WIDTH), BF16),
            jax.ShapeDtypeStruct((BATCH, SEQ // T_DIFF, T_WIDTH, T_DIFF), BF16),
        ],
        compiler_params=pltpu.CompilerParams(
            dimension_semantics=("arbitrary", "arbitrary"),
            vmem_limit_bytes=VMEM_LIMIT_BYTES),
        name="pre_mixer",
    )(x, gains, mod3, w_a, w_t, cos4, sin4, cos_t, sin_t)


def _diff_kernel(lam_ref, g_ref, qt_ref, k_ref, vt_ref, o_ref,
                 qbd_ref, m_ref, acc_ref, s0_ref, s1_ref, max0_ref, max1_ref, *, lam_init):
    pair_idx = pl.program_id(1)
    t = T_DIFF
    feat = lax.broadcasted_iota(jnp.int32, (LANES, 1), 0)
    first_map = ((feat // HALF) % 2) == 0

    chains = [(sub, h) for sub in range(2) for h in range(DIFF_HEADS)]
    sub_chains = lambda sub: [c for c, (sb, _) in enumerate(chains) if sb == sub]
    all_chains = list(range(len(chains)))

    for c, (sub, h) in enumerate(chains):
        qh = qt_ref[0, sub, h * LANES:(h + 1) * LANES, :]
        zero = jnp.zeros_like(qh)
        qbd_ref[c, :, 0:t] = jnp.where(first_map, qh, zero)
        qbd_ref[c, :, t:2 * t] = jnp.where(first_map, zero, qh)
        m_ref[c] = jnp.full((1, 2 * t), NEG, F32)
        acc_ref[c] = jnp.zeros((LANES + DENOM_ROWS, 2 * t), F32)
    ones_rows = jnp.ones((DENOM_ROWS, t), BF16)

    def scores(j, c, s_ref, max_ref):
        h = chains[c][1]
        start = pl.multiple_of(j * t, t)
        kh = k_ref[0, pl.ds(start, t), h * LANES:(h + 1) * LANES]
        s = jnp.dot(kh, qbd_ref[c], preferred_element_type=F32)
        s_ref[c] = s
        max_ref[c] = jnp.max(s, axis=0, keepdims=True)

    def accumulate(j, c, s_ref, max_ref, diagonal):
        h = chains[c][1]
        vth = vt_ref[0, j, h * LANES:(h + 1) * LANES, :]
        s = s_ref[c]
        if diagonal:
            key = lax.broadcasted_iota(jnp.int32, (t, 2 * t), 0)
            qry = lax.broadcasted_iota(jnp.int32, (t, 2 * t), 1) & (t - 1)
            s = jnp.where(key <= qry, s, NEG)
            m_blk = jnp.max(s, axis=0, keepdims=True)
        else:
            m_blk = max_ref[c]
        m_old = m_ref[c]
        m_new = jnp.maximum(m_old, m_blk)
        p = jnp.exp2(s - m_new)
        alpha = jnp.exp2(m_old - m_new)
        v_aug = jnp.concatenate([vth, ones_rows], axis=0)
        acc_ref[c] = alpha * acc_ref[c] + jnp.dot(v_aug, p.astype(BF16), preferred_element_type=F32)
        m_ref[c] = m_new

    for c in all_chains:
        scores(0, c, s0_ref, max0_ref)

    def two_blocks(step, carry):
        j = 2 * step
        for c in all_chains:
            scores(j + 1, c, s1_ref, max1_ref)
            accumulate(j, c, s0_ref, max0_ref, False)
        for c in all_chains:
            scores(j + 2, c, s0_ref, max0_ref)
            accumulate(j + 1, c, s1_ref, max1_ref, False)
        return carry

    lax.fori_loop(0, pair_idx, two_blocks, 0)

    lam_vec = lam_ref[...]
    dot1 = jnp.sum(lam_vec[0:1] * lam_vec[1:2], axis=-1, keepdims=True)
    dot2 = jnp.sum(lam_vec[2:3] * lam_vec[3:4], axis=-1, keepdims=True)
    lam = jnp.exp(dot1) - jnp.exp(dot2) + lam_init
    g = g_ref[...]

    def finalize(c):
        sub, h = chains[c]
        denom = acc_ref[c, LANES:LANES + 1, :]
        o_all = acc_ref[c, 0:LANES, :] * (1.0 / denom)
        o = o_all[:, 0:t] - lam * o_all[:, t:2 * t]
        o = o * lax.rsqrt(jnp.mean(o * o, axis=0, keepdims=True) + EPS)
        o = (o * g) * (1.0 - lam_init)
        o_ref[0, sub * t:(sub + 1) * t, h * LANES:(h + 1) * LANES] = o.T.astype(BF16)

    diag0 = 2 * pair_idx
    for c0, c1 in zip(sub_chains(0), sub_chains(1)):
        scores(diag0 + 1, c1, s1_ref, max1_ref)
        accumulate(diag0, c0, s0_ref, max0_ref, True)
        accumulate(diag0, c1, s0_ref, max0_ref, False)
    for c0, c1 in zip(sub_chains(0), sub_chains(1)):
        accumulate(diag0 + 1, c1, s1_ref, max1_ref, True)
        finalize(c0)
    for c1 in sub_chains(1):
        finalize(c1)


def _diff_call(layer, proj_a, proj_t, lam_vec, subln_g):
    lam_init = 0.8 - 0.6 * math.exp(-0.3 * layer)
    n_blocks = SEQ // T_DIFF
    n_chains = 2 * DIFF_HEADS
    kernel = functools.partial(_diff_kernel, lam_init=lam_init)
    return pl.pallas_call(
        kernel,
        grid=(BATCH, n_blocks // 2),
        in_specs=[
            pl.BlockSpec((4, HEAD_DIM), lambda b, i: (0, 0)),
            pl.BlockSpec((LANES, 1), lambda b, i: (0, 0)),
            pl.BlockSpec((1, 2, DIFF_WIDTH, T_DIFF), lambda b, i: (b, i, T_Q_DIFF // DIFF_WIDTH, 0)),
            pl.BlockSpec((1, SEQ, DIFF_WIDTH), lambda b, i: (b, 0, A_K_DIFF // DIFF_WIDTH)),
            pl.BlockSpec((1, n_blocks, DIFF_WIDTH, T_DIFF), lambda b, i: (b, 0, T_V_DIFF // DIFF_WIDTH, 0)),
        ],
        out_specs=pl.BlockSpec((1, 2 * T_DIFF, DIFF_WIDTH), lambda b, i: (b, i, 0)),
        out_shape=jax.ShapeDtypeStruct((BATCH, SEQ, DIFF_WIDTH), BF16),
        scratch_shapes=[
            pltpu.VMEM((n_chains, LANES, 2 * T_DIFF), BF16),
            pltpu.VMEM((n_chains, 1, 2 * T_DIFF), F32),
            pltpu.VMEM((n_chains, LANES + DENOM_ROWS, 2 * T_DIFF), F32),
            pltpu.VMEM((n_chains, T_DIFF, 2 * T_DIFF), F32),
            pltpu.VMEM((n_chains, T_DIFF, 2 * T_DIFF), F32),
            pltpu.VMEM((n_chains, 1, 2 * T_DIFF), F32),
            pltpu.VMEM((n_chains, 1, 2 * T_DIFF), F32),
        ],
        compiler_params=pltpu.CompilerParams(
            dimension_semantics=("arbitrary", "arbitrary"),
            vmem_limit_bytes=VMEM_LIMIT_BYTES),
        name="diff_attn",
    )(lam_vec, subln_g.reshape(LANES, 1), proj_t, proj_a, proj_t)


def _swa_kernel(sink_ref, qt_ref, kprev_ref, kcur_ref, vtprev_ref, vtcur_ref, o_ref,
                s_ref, pv_ref, m_ref, bias_ref):
    i = pl.program_id(1)
    w = WINDOW
    n_sub = TQ_SWA // w
    feat = lax.broadcasted_iota(jnp.int32, (LANES, 1), 0)
    kv0_feat = ((feat // HALF) % 2) == 0
    col = lax.broadcasted_iota(jnp.int32, (1, 2 * w), 1)
    first_cols = col < w

    @pl.when(i == 0)
    def _():
        key = lax.broadcasted_iota(jnp.int32, (2 * w, 2 * w), 0)
        qry = lax.broadcasted_iota(jnp.int32, (2 * w, 2 * w), 1) & (w - 1)
        in_window = (key > qry) & (key <= qry + w)
        bias_ref[0] = jnp.where(in_window, 0.0, NEG)
        bias_ref[1] = jnp.where(in_window & (key >= w), 0.0, NEG)
    ones_rows = jnp.ones((DENOM_ROWS, 2 * w), BF16)

    def band(n):
        tile, half = divmod(n, T_DIFF // w)
        if n == 0:
            k_band = jnp.concatenate([kprev_ref[0], kcur_ref[0, 0:w, :]], axis=0)
            vt_band = jnp.concatenate([vtprev_ref[0, 0, :, w:2 * w], vtcur_ref[0, 0, :, 0:w]], axis=1)
            return k_band, vt_band, bias_ref[jnp.where(i == 0, 1, 0)]
        k_band = kcur_ref[0, (n - 1) * w:(n + 1) * w, :]
        if half == 1:
            vt_band = vtcur_ref[0, tile]
        else:
            vt_band = jnp.concatenate([vtcur_ref[0, tile - 1, :, w:2 * w], vtcur_ref[0, tile, :, 0:w]], axis=1)
        return k_band, vt_band, bias_ref[0]

    def scores(n, g):
        tile, half = divmod(n, T_DIFF // w)
        k_band, _, bias = band(n)
        qg = qt_ref[0, tile, g * LANES:(g + 1) * LANES, half * w:(half + 1) * w]
        zero = jnp.zeros_like(qg)
        qbd = jnp.concatenate([jnp.where(kv0_feat, qg, zero), jnp.where(kv0_feat, zero, qg)], axis=1)
        s = jnp.dot(k_band, qbd, preferred_element_type=F32)
        s_ref[n * SWA_GROUP + g] = s + bias

    def sink_row(g):
        return jnp.where(first_cols, sink_ref[g], sink_ref[SWA_GROUP + g]) * LOG2E

    def attend(n, g):
        c = n * SWA_GROUP + g
        _, vt_band, _ = band(n)
        v_aug = jnp.concatenate([vt_band, ones_rows], axis=0)
        s = s_ref[c]
        m = jnp.maximum(jnp.max(s, axis=0, keepdims=True), sink_row(g))
        p = jnp.exp2(s - m)
        pv_ref[c] = jnp.dot(v_aug, p.astype(BF16), preferred_element_type=F32)
        m_ref[c] = m

    def finish(n, g):
        c = n * SWA_GROUP + g
        pv = pv_ref[c]
        inv = 1.0 / (pv[LANES:LANES + 1, :] + jnp.exp2(sink_row(g) - m_ref[c]))
        o_pair = jnp.concatenate([pv[0:HEAD_DIM, 0:w] * inv[:, 0:w],
                                  pv[HEAD_DIM:LANES, w:2 * w] * inv[:, w:2 * w]], axis=0)
        o_ref[0, n * w:(n + 1) * w, g * LANES:(g + 1) * LANES] = o_pair.T.astype(BF16)

    order = [(n, g) for n in range(n_sub) for g in range(SWA_GROUP)]
    stages = (scores, attend, finish)
    group = SWA_PIPE_GROUP
    groups = [order[k:k + group] for k in range(0, len(order), group)]
    for tick in range(len(groups) + len(stages) - 1):
        for lag, stage in enumerate(stages):
            if 0 <= tick - lag < len(groups):
                for chain in groups[tick - lag]:
                    stage(*chain)


def _swa_call(proj_a, proj_t, sinks):
    n_q = SEQ // TQ_SWA
    tiles = TQ_SWA // T_DIFF
    subs = TQ_SWA // WINDOW
    return pl.pallas_call(
        _swa_kernel,
        grid=(BATCH, n_q),
        in_specs=[
            pl.BlockSpec(memory_space=pltpu.SMEM),
            pl.BlockSpec((1, tiles, SWA_WIDTH, T_DIFF), lambda b, i: (b, i, T_Q_SWA // SWA_WIDTH, 0)),
            pl.BlockSpec((1, WINDOW, LANES), lambda b, i: (b, jnp.maximum(i * subs - 1, 0), A_K_SWA // LANES)),
            pl.BlockSpec((1, TQ_SWA, LANES), lambda b, i: (b, i, A_K_SWA // LANES)),
            pl.BlockSpec((1, 1, LANES, T_DIFF), lambda b, i: (b, jnp.maximum(i * tiles - 1, 0), T_V_SWA // LANES, 0)),
            pl.BlockSpec((1, tiles, LANES, T_DIFF), lambda b, i: (b, i, T_V_SWA // LANES, 0)),
        ],
        out_specs=pl.BlockSpec((1, TQ_SWA, SWA_WIDTH), lambda b, i: (b, i, 0)),
        out_shape=jax.ShapeDtypeStruct((BATCH, SEQ, SWA_WIDTH), BF16),
        scratch_shapes=[
            pltpu.VMEM((subs * SWA_GROUP, 2 * WINDOW, 2 * WINDOW), F32),
            pltpu.VMEM((subs * SWA_GROUP, LANES + DENOM_ROWS, 2 * WINDOW), F32),
            pltpu.VMEM((subs * SWA_GROUP, 1, 2 * WINDOW), F32),
            pltpu.VMEM((2, 2 * WINDOW, 2 * WINDOW), F32),
        ],
        compiler_params=pltpu.CompilerParams(
            dimension_semantics=("arbitrary", "arbitrary"),
            vmem_limit_bytes=VMEM_LIMIT_BYTES),
        name="swa_attn",
    )(sinks, proj_t, proj_a, proj_a, proj_t, proj_t)


def _post_kernel(x_ref, od_ref, os_ref, gain_ref, mod_ref, wo_ref, wg_ref, wu_ref, wd_ref, o_ref):
    g_mix_post, g_ffn_pre, g_ffn_post = (_row(gain_ref, k) for k in (GAIN_MIX_POST, GAIN_FFN_PRE, GAIN_FFN_POST))
    gate1, shift2, scale2, gate2 = (_row(mod_ref, k) for k in (MOD_GT1, MOD_SH2, MOD_SC2, MOD_GT2))

    def mixer_residual(rows):
        x = x_ref[0, rows, :]
        y = jnp.dot(od_ref[0, rows, :], wo_ref[0, 0:DIFF_WIDTH, :], preferred_element_type=F32)
        y = y + jnp.dot(os_ref[0, rows, :], wo_ref[0, DIFF_WIDTH:D_MODEL, :], preferred_element_type=F32)
        x1 = x + gate1 * (_rms(y) * g_mix_post)
        h = (_rms(x1) * g_ffn_pre) * (1.0 + scale2) + shift2
        return x1, h.astype(BF16)

    def ffn_residual(x1, hb):
        gate = jnp.dot(hb, wg_ref[0], preferred_element_type=F32)
        up = jnp.dot(hb, wu_ref[0], preferred_element_type=F32)
        act = (gate / (1.0 + jnp.exp(-gate))) * up
        y2 = jnp.dot(act.astype(BF16), wd_ref[0], preferred_element_type=F32)
        return x1 + gate2 * (_rms(y2) * g_ffn_post)

    subs = [slice(k * SUB_POST, (k + 1) * SUB_POST) for k in range(TM_POST // SUB_POST)]
    staged = [mixer_residual(rows) for rows in subs]
    for rows, (x1, hb) in zip(subs, staged):
        o_ref[0, rows, :] = ffn_residual(x1, hb)


def _post_call(layer, x, o_diff, o_swa, mod3, gains, w_out_b, w_gate_b, w_up_b, w_down_b):
    n_t = SEQ // TM_POST
    return pl.pallas_call(
        _post_kernel,
        grid=(BATCH, n_t),
        in_specs=[
            pl.BlockSpec((1, TM_POST, D_MODEL), lambda b, i: (b, i, 0)),
            pl.BlockSpec((1, TM_POST, DIFF_WIDTH), lambda b, i: (b, i, 0)),
            pl.BlockSpec((1, TM_POST, SWA_WIDTH), lambda b, i: (b, i, 0)),
            _gain_spec(layer),
            _mod_spec(layer),
            _layer_weight(layer, (D_MODEL, D_MODEL)),
            _layer_weight(layer, (D_MODEL, D_FF)),
            _layer_weight(layer, (D_MODEL, D_FF)),
            _layer_weight(layer, (D_FF, D_MODEL)),
        ],
        out_specs=pl.BlockSpec((1, TM_POST, D_MODEL), lambda b, i: (b, i, 0)),
        out_shape=jax.ShapeDtypeStruct((BATCH, SEQ, D_MODEL), F32),
        compiler_params=pltpu.CompilerParams(
            dimension_semantics=("arbitrary", "arbitrary"),
            vmem_limit_bytes=VMEM_LIMIT_BYTES),
        name="post_ffn",
    )(x, o_diff, o_swa, gains, mod3, w_out_b, w_gate_b, w_up_b, w_down_b)


def kernel(x, c, ada_w, ada_b, g_mix_pre, g_mix_post, g_ffn_pre, g_ffn_post, w_in, lambda_q1, lambda_k1,
           lambda_q2, lambda_k2, subln_g, sinks, w_out, w_gate, w_up, w_down):
    tables = _rope_tables()

    mod = _ada_call(c, ada_w, ada_b)
    mod3 = mod.reshape(DEPTH * BATCH, 6, D_MODEL)
    gains = jnp.stack([g_mix_pre, g_mix_post, g_ffn_pre, g_ffn_post], axis=1)

    w_a, w_t = _projection_weights(w_in)
    w_out_b = _out_proj_weight(w_out)
    w_gate_b, w_up_b, w_down_b = w_gate.astype(BF16), w_up.astype(BF16), w_down.astype(BF16)

    for layer in range(DEPTH):
        lam_vec = jnp.stack([lambda_q1[layer], lambda_k1[layer], lambda_q2[layer], lambda_k2[layer]])
        proj_a, proj_t = _pre_call(layer, x, mod3, gains, w_a, w_t, tables)
        o_diff = _diff_call(layer, proj_a, proj_t, lam_vec, subln_g[layer])
        o_swa = _swa_call(proj_a, proj_t, sinks[layer])
        x = _post_call(layer, x, o_diff, o_swa, mod3, gains, w_out_b, w_gate_b, w_up_b, w_down_b)
    return x
```

```python
import functools
import math

import numpy as np
import jax
import jax.numpy as jnp
from jax import lax
from jax.experimental import pallas as pl
from jax.experimental.pallas import tpu as pltpu

D_MODEL = 1024
BATCH = 16
SEQ = 2048
DEPTH = 2
HEAD_DIM = 64
HALF = HEAD_DIM // 2
DIFF_HEADS = 4
DIFF_WIDTH = 512
SWA_HEADS = 8
SWA_KV_HEADS = 2
SWA_GROUP = SWA_HEADS // SWA_KV_HEADS
SWA_WIDTH = 512
WINDOW = 128
ROPE_THETA = 10000.0
D_FF = 2816
EPS = 1e-6
NEG = -1e30
LOG2E = math.log2(math.e)

LANES = 128

REF_Q_DIFF, REF_K_DIFF, REF_V_DIFF, REF_Q_SWA, REF_K_SWA, REF_V_SWA = 0, 512, 1024, 1536, 2048, 2176
A_K_DIFF, A_K_SWA, A_WIDTH = 0, 512, 640
T_Q_DIFF, T_V_DIFF, T_Q_SWA, T_V_SWA, T_WIDTH = 0, 512, 1024, 1536, 1664
T_ROPE_ROWS = ((T_Q_DIFF, DIFF_WIDTH), (T_Q_SWA, SWA_WIDTH))

VMEM_LIMIT_BYTES = 56 * 1024 * 1024

TM_PRE = 512
TM_POST = 1024
SUB_POST = 256
T_DIFF = 256
DENOM_ROWS = 16
TQ_SWA = 2048
SWA_PIPE_GROUP = 4

F32 = jnp.float32
BF16 = jnp.bfloat16
NT_DIMS = (((1,), (1,)), ((), ()))


def _projection_weights(w_in):
    def section(start, width):
        return w_in[:, :, start:start + width]

    def halves_outer(w, pairs):
        w = w.reshape(DEPTH, D_MODEL, pairs, 2, 2, HALF)
        return jnp.swapaxes(w, 3, 4).reshape(DEPTH, D_MODEL, pairs * LANES)

    score_scale = HEAD_DIM ** -0.5
    q_diff = halves_outer(section(REF_Q_DIFF, DIFF_WIDTH), DIFF_HEADS) * score_scale
    k_diff = halves_outer(section(REF_K_DIFF, DIFF_WIDTH), DIFF_HEADS)
    v_diff = section(REF_V_DIFF, DIFF_WIDTH)
    q_swa = section(REF_Q_SWA, SWA_WIDTH).reshape(DEPTH, D_MODEL, SWA_KV_HEADS, SWA_GROUP, 2, HALF)
    q_swa = q_swa.transpose(0, 1, 3, 4, 2, 5).reshape(DEPTH, D_MODEL, SWA_WIDTH) * score_scale
    k_swa = halves_outer(section(REF_K_SWA, LANES), 1)
    v_swa = section(REF_V_SWA, LANES)
    w_a = jnp.concatenate([k_diff, k_swa], axis=-1).astype(BF16)
    w_t = jnp.swapaxes(jnp.concatenate([q_diff, v_diff, q_swa, v_swa], axis=-1), 1, 2).astype(BF16)
    return w_a, w_t


def _out_proj_weight(w_out):
    diff_rows = w_out[:, 0:DIFF_WIDTH, :]
    swa_rows = w_out[:, DIFF_WIDTH:, :].reshape(DEPTH, SWA_KV_HEADS, SWA_GROUP, HEAD_DIM, D_MODEL)
    swa_rows = jnp.swapaxes(swa_rows, 1, 2).reshape(DEPTH, SWA_WIDTH, D_MODEL)
    return jnp.concatenate([diff_rows, swa_rows], axis=1).astype(BF16)


def _rope_tables():
    pos = jnp.arange(SEQ, dtype=F32)
    inv = ROPE_THETA ** (-jnp.arange(0, HEAD_DIM, 2, dtype=F32) / HEAD_DIM)
    ang = pos[:, None] * inv[None, :]
    cos, sin = jnp.cos(ang), jnp.sin(ang)
    cos4 = jnp.concatenate([cos, cos, cos, cos], axis=-1)
    sin4 = jnp.concatenate([-sin, -sin, sin, sin], axis=-1)
    cos_t = jnp.concatenate([cos, cos], axis=-1).T * LOG2E
    sin_t = jnp.concatenate([sin, sin], axis=-1).T * LOG2E
    return cos4, sin4, cos_t, sin_t


def _rms(x):
    return x * lax.rsqrt(jnp.mean(x * x, axis=-1, keepdims=True) + EPS)


def _ada_kernel(c_ref, w_ref, b_ref, o_ref):
    c = c_ref[...]
    c_act = c / (1.0 + jnp.exp(-c))
    o_ref[0] = jnp.dot(c_act, w_ref[0], preferred_element_type=F32) + b_ref[0]


def _ada_call(c, ada_w, ada_b):
    n_chunk = 6
    return pl.pallas_call(
        _ada_kernel,
        grid=(DEPTH, n_chunk),
        in_specs=[
            pl.BlockSpec((BATCH, D_MODEL), lambda l, k: (0, 0)),
            pl.BlockSpec((1, D_MODEL, D_MODEL), lambda l, k: (l, 0, k)),
            pl.BlockSpec((1, 1, D_MODEL), lambda l, k: (l, 0, k)),
        ],
        out_specs=pl.BlockSpec((1, BATCH, D_MODEL), lambda l, k: (l, 0, k)),
        out_shape=jax.ShapeDtypeStruct((DEPTH, BATCH, 6 * D_MODEL), F32),
        name="ada_mod",
    )(c, ada_w, ada_b.reshape(DEPTH, 1, 6 * D_MODEL))


def _pre_kernel(x_ref, gain_ref, mod_ref, wa_ref, wt_ref, cos_ref, sin_ref, cost_ref, sint_ref,
                oa_ref, ot_ref):
    gain = _row(gain_ref, GAIN_MIX_PRE)
    shift, scale = _row(mod_ref, MOD_SH1), _row(mod_ref, MOD_SC1)

    def modulate(tok):
        x = x_ref[0, tok, :]
        h = (_rms(x) * gain) * (1.0 + scale) + shift
        return h.astype(BF16)

    def project(hb):
        return (jnp.dot(hb, wa_ref[0], preferred_element_type=F32),
                lax.dot_general(wt_ref[0], hb, NT_DIMS, preferred_element_type=F32))

    def rotate_store(blk, tok, proj, proj_t):
        cos = cos_ref[tok, :]
        sin = sin_ref[tok, :]
        for grp in range(A_WIDTH // LANES):
            cols = slice(grp * LANES, (grp + 1) * LANES)
            xg = proj[:, cols]
            oa_ref[0, tok, cols] = (xg * cos + pltpu.roll(xg, 64, 1) * sin).astype(BF16)

        cos_t = cost_ref[:, tok]
        sin_t = sint_ref[:, tok]
        for start, size in T_ROPE_ROWS:
            for grp in range(size // LANES):
                base = start + grp * LANES
                lo = proj_t[base:base + 64]
                hi = proj_t[base + 64:base + LANES]
                ot_ref[0, blk, base:base + 64, :] = (lo * cos_t - hi * sin_t).astype(BF16)
                ot_ref[0, blk, base + 64:base + LANES, :] = (hi * cos_t + lo * sin_t).astype(BF16)
        for start, size in ((T_V_DIFF, DIFF_WIDTH), (T_V_SWA, LANES)):
            ot_ref[0, blk, start:start + size, :] = proj_t[start:start + size].astype(BF16)

    toks = [slice(blk * T_DIFF, (blk + 1) * T_DIFF) for blk in range(TM_PRE // T_DIFF)]
    staged = [modulate(tok) for tok in toks]
    projected = [project(hb) for hb in staged]
    for blk, (tok, (proj, proj_t)) in enumerate(zip(toks, projected)):
        rotate_store(blk, tok, proj, proj_t)


MOD_SH1, MOD_SC1, MOD_GT1, MOD_SH2, MOD_SC2, MOD_GT2 = range(6)
GAIN_MIX_PRE, GAIN_MIX_POST, GAIN_FFN_PRE, GAIN_FFN_POST = range(4)


def _mod_spec(layer):
    return pl.BlockSpec((1, 6, D_MODEL), lambda b, i: (layer * BATCH + b, 0, 0))


def _gain_spec(layer):
    return pl.BlockSpec((1, 4, D_MODEL), lambda b, i: (layer, 0, 0))


def _row(ref, k):
    return ref[0, k:k + 1, :]


def _layer_weight(layer, shape):
    return pl.BlockSpec((1,) + shape, lambda b, i: (layer,) + (0,) * len(shape), pipeline_mode=pl.Buffered(1))


def _pre_call(layer, x, mod3, gains, w_a, w_t, tables):
    cos4, sin4, cos_t, sin_t = tables
    n_t = SEQ // TM_PRE
    n_blk = TM_PRE // T_DIFF
    return pl.pallas_call(
        _pre_kernel,
        grid=(BATCH, n_t),
        in_specs=[
            pl.BlockSpec((1, TM_PRE, D_MODEL), lambda b, i: (b, i, 0)),
            _gain_spec(layer),
            _mod_spec(layer),
            _layer_weight(layer, (D_MODEL, A_WIDTH)),
            _layer_weight(layer, (T_WIDTH, D_MODEL)),
            pl.BlockSpec((TM_PRE, LANES), lambda b, i: (i, 0)),
            pl.BlockSpec((TM_PRE, LANES), lambda b, i: (i, 0)),
            pl.BlockSpec((HEAD_DIM, TM_PRE), lambda b, i: (0, i)),
            pl.BlockSpec((HEAD_DIM, TM_PRE), lambda b, i: (0, i)),
        ],
        out_specs=[
            pl.BlockSpec((1, TM_PRE, A_WIDTH), lambda b, i: (b, i, 0)),
            pl.BlockSpec((1, n_blk, T_WIDTH, T_DIFF), lambda b, i: (b, i, 0, 0)),
        ],
        out_shape=[
            jax.ShapeDtypeStruct((BATCH, SEQ, A_WIDTH), BF16),
            jax.ShapeDtypeStruct((BATCH, SEQ // T_DIFF, T_WIDTH, T_DIFF), BF16),
        ],
        compiler_params=pltpu.CompilerParams(
            dimension_semantics=("arbitrary", "arbitrary"),
            vmem_limit_bytes=VMEM_LIMIT_BYTES),
        name="pre_mixer",
    )(x, gains, mod3, w_a, w_t, cos4, sin4, cos_t, sin_t)


def _diff_kernel(lam_ref, g_ref, qt_ref, k_ref, vt_ref, o_ref,
                 qbd_ref, m_ref, acc_ref, s0_ref, s1_ref, max0_ref, max1_ref, *, lam_init):
    pair_idx = pl.program_id(1)
    t = T_DIFF
    feat = lax.broadcasted_iota(jnp.int32, (LANES, 1), 0)
    first_map = ((feat // HALF) % 2) == 0

    chains = [(sub, h) for sub in range(2) for h in range(DIFF_HEADS)]
    sub_chains = lambda sub: [c for c, (sb, _) in enumerate(chains) if sb == sub]
    all_chains = list(range(len(chains)))

    for c, (sub, h) in enumerate(chains):
        qh = qt_ref[0, sub, h * LANES:(h + 1) * LANES, :]
        zero = jnp.zeros_like(qh)
        qbd_ref[c, :, 0:t] = jnp.where(first_map, qh, zero)
        qbd_ref[c, :, t:2 * t] = jnp.where(first_map, zero, qh)
        m_ref[c] = jnp.full((1, 2 * t), NEG, F32)
        acc_ref[c] = jnp.zeros((LANES + DENOM_ROWS, 2 * t), F32)
    ones_rows = jnp.ones((DENOM_ROWS, t), BF16)

    def scores(j, c, s_ref, max_ref):
        h = chains[c][1]
        start = pl.multiple_of(j * t, t)
        kh = k_ref[0, pl.ds(start, t), h * LANES:(h + 1) * LANES]
        s = jnp.dot(kh, qbd_ref[c], preferred_element_type=F32)
        s_ref[c] = s
        max_ref[c] = jnp.max(s, axis=0, keepdims=True)

    def accumulate(j, c, s_ref, max_ref, diagonal):
        h = chains[c][1]
        vth = vt_ref[0, j, h * LANES:(h + 1) * LANES, :]
        s = s_ref[c]
        if diagonal:
            key = lax.broadcasted_iota(jnp.int32, (t, 2 * t), 0)
            qry = lax.broadcasted_iota(jnp.int32, (t, 2 * t), 1) & (t - 1)
            s = jnp.where(key <= qry, s, NEG)
            m_blk = jnp.max(s, axis=0, keepdims=True)
        else:
            m_blk = max_ref[c]
        m_old = m_ref[c]
        m_new = jnp.maximum(m_old, m_blk)
        p = jnp.exp2(s - m_new)
        alpha = jnp.exp2(m_old - m_new)
        v_aug = jnp.concatenate([vth, ones_rows], axis=0)
        acc_ref[c] = alpha * acc_ref[c] + jnp.dot(v_aug, p.astype(BF16), preferred_element_type=F32)
        m_ref[c] = m_new

    for c in all_chains:
        scores(0, c, s0_ref, max0_ref)

    def two_blocks(step, carry):
        j = 2 * step
        for c in all_chains:
            scores(j + 1, c, s1_ref, max1_ref)
            accumulate(j, c, s0_ref, max0_ref, False)
        for c in all_chains:
            scores(j + 2, c, s0_ref, max0_ref)
            accumulate(j + 1, c, s1_ref, max1_ref, False)
        return carry

    lax.fori_loop(0, pair_idx, two_blocks, 0)

    lam_vec = lam_ref[...]
    dot1 = jnp.sum(lam_vec[0:1] * lam_vec[1:2], axis=-1, keepdims=True)
    dot2 = jnp.sum(lam_vec[2:3] * lam_vec[3:4], axis=-1, keepdims=True)
    lam = jnp.exp(dot1) - jnp.exp(dot2) + lam_init
    g = g_ref[...]

    def finalize(c):
        sub, h = chains[c]
        denom = acc_ref[c, LANES:LANES + 1, :]
        o_all = acc_ref[c, 0:LANES, :] * (1.0 / denom)
        o = o_all[:, 0:t] - lam * o_all[:, t:2 * t]
        o = o * lax.rsqrt(jnp.mean(o * o, axis=0, keepdims=True) + EPS)
        o = (o * g) * (1.0 - lam_init)
        o_ref[0, sub * t:(sub + 1) * t, h * LANES:(h + 1) * LANES] = o.T.astype(BF16)

    diag0 = 2 * pair_idx
    for c0, c1 in zip(sub_chains(0), sub_chains(1)):
        scores(diag0 + 1, c1, s1_ref, max1_ref)
        accumulate(diag0, c0, s0_ref, max0_ref, True)
        accumulate(diag0, c1, s0_ref, max0_ref, False)
    for c0, c1 in zip(sub_chains(0), sub_chains(1)):
        accumulate(diag0 + 1, c1, s1_ref, max1_ref, True)
        finalize(c0)
    for c1 in sub_chains(1):
        finalize(c1)


def _diff_call(layer, proj_a, proj_t, lam_vec, subln_g):
    lam_init = 0.8 - 0.6 * math.exp(-0.3 * layer)
    n_blocks = SEQ // T_DIFF
    n_chains = 2 * DIFF_HEADS
    kernel = functools.partial(_diff_kernel, lam_init=lam_init)
    return pl.pallas_call(
        kernel,
        grid=(BATCH, n_blocks // 2),
        in_specs=[
            pl.BlockSpec((4, HEAD_DIM), lambda b, i: (0, 0)),
            pl.BlockSpec((LANES, 1), lambda b, i: (0, 0)),
            pl.BlockSpec((1, 2, DIFF_WIDTH, T_DIFF), lambda b, i: (b, i, T_Q_DIFF // DIFF_WIDTH, 0)),
            pl.BlockSpec((1, SEQ, DIFF_WIDTH), lambda b, i: (b, 0, A_K_DIFF // DIFF_WIDTH)),
            pl.BlockSpec((1, n_blocks, DIFF_WIDTH, T_DIFF), lambda b, i: (b, 0, T_V_DIFF // DIFF_WIDTH, 0)),
        ],
        out_specs=pl.BlockSpec((1, 2 * T_DIFF, DIFF_WIDTH), lambda b, i: (b, i, 0)),
        out_shape=jax.ShapeDtypeStruct((BATCH, SEQ, DIFF_WIDTH), BF16),
        scratch_shapes=[
            pltpu.VMEM((n_chains, LANES, 2 * T_DIFF), BF16),
            pltpu.VMEM((n_chains, 1, 2 * T_DIFF), F32),
            pltpu.VMEM((n_chains, LANES + DENOM_ROWS, 2 * T_DIFF), F32),
            pltpu.VMEM((n_chains, T_DIFF, 2 * T_DIFF), F32),
            pltpu.VMEM((n_chains, T_DIFF, 2 * T_DIFF), F32),
            pltpu.VMEM((n_chains, 1, 2 * T_DIFF), F32),
            pltpu.VMEM((n_chains, 1, 2 * T_DIFF), F32),
        ],
        compiler_params=pltpu.CompilerParams(
            dimension_semantics=("arbitrary", "arbitrary"),
            vmem_limit_bytes=VMEM_LIMIT_BYTES),
        name="diff_attn",
    )(lam_vec, subln_g.reshape(LANES, 1), proj_t, proj_a, proj_t)


def _swa_kernel(sink_ref, qt_ref, kprev_ref, kcur_ref, vtprev_ref, vtcur_ref, o_ref,
                s_ref, pv_ref, m_ref, bias_ref):
    i = pl.program_id(1)
    w = WINDOW
    n_sub = TQ_SWA // w
    feat = lax.broadcasted_iota(jnp.int32, (LANES, 1), 0)
    kv0_feat = ((feat // HALF) % 2) == 0
    col = lax.broadcasted_iota(jnp.int32, (1, 2 * w), 1)
    first_cols = col < w

    @pl.when(i == 0)
    def _():
        key = lax.broadcasted_iota(jnp.int32, (2 * w, 2 * w), 0)
        qry = lax.broadcasted_iota(jnp.int32, (2 * w, 2 * w), 1) & (w - 1)
        in_window = (key > qry) & (key <= qry + w)
        bias_ref[0] = jnp.where(in_window, 0.0, NEG)
        bias_ref[1] = jnp.where(in_window & (key >= w), 0.0, NEG)
    ones_rows = jnp.ones((DENOM_ROWS, 2 * w), BF16)

    def band(n):
        tile, half = divmod(n, T_DIFF // w)
        if n == 0:
            k_band = jnp.concatenate([kprev_ref[0], kcur_ref[0, 0:w, :]], axis=0)
            vt_band = jnp.concatenate([vtprev_ref[0, 0, :, w:2 * w], vtcur_ref[0, 0, :, 0:w]], axis=1)
            return k_band, vt_band, bias_ref[jnp.where(i == 0, 1, 0)]
        k_band = kcur_ref[0, (n - 1) * w:(n + 1) * w, :]
        if half == 1:
            vt_band = vtcur_ref[0, tile]
        else:
            vt_band = jnp.concatenate([vtcur_ref[0, tile - 1, :, w:2 * w], vtcur_ref[0, tile, :, 0:w]], axis=1)
        return k_band, vt_band, bias_ref[0]

    def scores(n, g):
        tile, half = divmod(n, T_DIFF // w)
        k_band, _, bias = band(n)
        qg = qt_ref[0, tile, g * LANES:(g + 1) * LANES, half * w:(half + 1) * w]
        zero = jnp.zeros_like(qg)
        qbd = jnp.concatenate([jnp.where(kv0_feat, qg, zero), jnp.where(kv0_feat, zero, qg)], axis=1)
        s = jnp.dot(k_band, qbd, preferred_element_type=F32)
        s_ref[n * SWA_GROUP + g] = s + bias

    def sink_row(g):
        return jnp.where(first_cols, sink_ref[g], sink_ref[SWA_GROUP + g]) * LOG2E

    def attend(n, g):
        c = n * SWA_GROUP + g
        _, vt_band, _ = band(n)
        v_aug = jnp.concatenate([vt_band, ones_rows], axis=0)
        s = s_ref[c]
        m = jnp.maximum(jnp.max(s, axis=0, keepdims=True), sink_row(g))
        p = jnp.exp2(s - m)
        pv_ref[c] = jnp.dot(v_aug, p.astype(BF16), preferred_element_type=F32)
        m_ref[c] = m

    def finish(n, g):
        c = n * SWA_GROUP + g
        pv = pv_ref[c]
        inv = 1.0 / (pv[LANES:LANES + 1, :] + jnp.exp2(sink_row(g) - m_ref[c]))
        o_pair = jnp.concatenate([pv[0:HEAD_DIM, 0:w] * inv[:, 0:w],
                                  pv[HEAD_DIM:LANES, w:2 * w] * inv[:, w:2 * w]], axis=0)
        o_ref[0, n * w:(n + 1) * w, g * LANES:(g + 1) * LANES] = o_pair.T.astype(BF16)

    order = [(n, g) for n in range(n_sub) for g in range(SWA_GROUP)]
    stages = (scores, attend, finish)
    group = SWA_PIPE_GROUP
    groups = [order[k:k + group] for k in range(0, len(order), group)]
    for tick in range(len(groups) + len(stages) - 1):
        for lag, stage in enumerate(stages):
            if 0 <= tick - lag < len(groups):
                for chain in groups[tick - lag]:
                    stage(*chain)


def _swa_call(proj_a, proj_t, sinks):
    n_q = SEQ // TQ_SWA
    tiles = TQ_SWA // T_DIFF
    subs = TQ_SWA // WINDOW
    return pl.pallas_call(
        _swa_kernel,
        grid=(BATCH, n_q),
        in_specs=[
            pl.BlockSpec(memory_space=pltpu.SMEM),
            pl.BlockSpec((1, tiles, SWA_WIDTH, T_DIFF), lambda b, i: (b, i, T_Q_SWA // SWA_WIDTH, 0)),
            pl.BlockSpec((1, WINDOW, LANES), lambda b, i: (b, jnp.maximum(i * subs - 1, 0), A_K_SWA // LANES)),
            pl.BlockSpec((1, TQ_SWA, LANES), lambda b, i: (b, i, A_K_SWA // LANES)),
            pl.BlockSpec((1, 1, LANES, T_DIFF), lambda b, i: (b, jnp.maximum(i * tiles - 1, 0), T_V_SWA // LANES, 0)),
            pl.BlockSpec((1, tiles, LANES, T_DIFF), lambda b, i: (b, i, T_V_SWA // LANES, 0)),
        ],
        out_specs=pl.BlockSpec((1, TQ_SWA, SWA_WIDTH), lambda b, i: (b, i, 0)),
        out_shape=jax.ShapeDtypeStruct((BATCH, SEQ, SWA_WIDTH), BF16),
        scratch_shapes=[
            pltpu.VMEM((subs * SWA_GROUP, 2 * WINDOW, 2 * WINDOW), F32),
            pltpu.VMEM((subs * SWA_GROUP, LANES + DENOM_ROWS, 2 * WINDOW), F32),
            pltpu.VMEM((subs * SWA_GROUP, 1, 2 * WINDOW), F32),
            pltpu.VMEM((2, 2 * WINDOW, 2 * WINDOW), F32),
        ],
        compiler_params=pltpu.CompilerParams(
            dimension_semantics=("arbitrary", "arbitrary"),
            vmem_limit_bytes=VMEM_LIMIT_BYTES),
        name="swa_attn",
    )(sinks, proj_t, proj_a, proj_a, proj_t, proj_t)


def _post_kernel(x_ref, od_ref, os_ref, gain_ref, mod_ref, wo_ref, wg_ref, wu_ref, wd_ref, o_ref):
    g_mix_post, g_ffn_pre, g_ffn_post = (_row(gain_ref, k) for k in (GAIN_MIX_POST, GAIN_FFN_PRE, GAIN_FFN_POST))
    gate1, shift2, scale2, gate2 = (_row(mod_ref, k) for k in (MOD_GT1, MOD_SH2, MOD_SC2, MOD_GT2))

    def mixer_residual(rows):
        x = x_ref[0, rows, :]
        y = jnp.dot(od_ref[0, rows, :], wo_ref[0, 0:DIFF_WIDTH, :], preferred_element_type=F32)
        y = y + jnp.dot(os_ref[0, rows, :], wo_ref[0, DIFF_WIDTH:D_MODEL, :], preferred_element_type=F32)
        x1 = x + gate1 * (_rms(y) * g_mix_post)
        h = (_rms(x1) * g_ffn_pre) * (1.0 + scale2) + shift2
        return x1, h.astype(BF16)

    def ffn_residual(x1, hb):
        gate = jnp.dot(hb, wg_ref[0], preferred_element_type=F32)
        up = jnp.dot(hb, wu_ref[0], preferred_element_type=F32)
        act = (gate / (1.0 + jnp.exp(-gate))) * up
        y2 = jnp.dot(act.astype(BF16), wd_ref[0], preferred_element_type=F32)
        return x1 + gate2 * (_rms(y2) * g_ffn_post)

    subs = [slice(k * SUB_POST, (k + 1) * SUB_POST) for k in range(TM_POST // SUB_POST)]
    staged = [mixer_residual(rows) for rows in subs]
    for rows, (x1, hb) in zip(subs, staged):
        o_ref[0, rows, :] = ffn_residual(x1, hb)


def _post_call(layer, x, o_diff, o_swa, mod3, gains, w_out_b, w_gate_b, w_up_b, w_down_b):
    n_t = SEQ // TM_POST
    return pl.pallas_call(
        _post_kernel,
        grid=(BATCH, n_t),
        in_specs=[
            pl.BlockSpec((1, TM_POST, D_MODEL), lambda b, i: (b, i, 0)),
            pl.BlockSpec((1, TM_POST, DIFF_WIDTH), lambda b, i: (b, i, 0)),
            pl.BlockSpec((1, TM_POST, SWA_WIDTH), lambda b, i: (b, i, 0)),
            _gain_spec(layer),
            _mod_spec(layer),
            _layer_weight(layer, (D_MODEL, D_MODEL)),
            _layer_weight(layer, (D_MODEL, D_FF)),
            _layer_weight(layer, (D_MODEL, D_FF)),
            _layer_weight(layer, (D_FF, D_MODEL)),
        ],
        out_specs=pl.BlockSpec((1, TM_POST, D_MODEL), lambda b, i: (b, i, 0)),
        out_shape=jax.ShapeDtypeStruct((BATCH, SEQ, D_MODEL), F32),
        compiler_params=pltpu.CompilerParams(
            dimension_semantics=("arbitrary", "arbitrary"),
            vmem_limit_bytes=VMEM_LIMIT_BYTES),
        name="post_ffn",
    )(x, o_diff, o_swa, gains, mod3, w_out_b, w_gate_b, w_up_b, w_down_b)


def kernel(x, c, ada_w, ada_b, g_mix_pre, g_mix_post, g_ffn_pre, g_ffn_post, w_in, lambda_q1, lambda_k1,
           lambda_q2, lambda_k2, subln_g, sinks, w_out, w_gate, w_up, w_down):
    tables = _rope_tables()

    mod = _ada_call(c, ada_w, ada_b)
    mod3 = mod.reshape(DEPTH * BATCH, 6, D_MODEL)
    gains = jnp.stack([g_mix_pre, g_mix_post, g_ffn_pre, g_ffn_post], axis=1)

    w_a, w_t = _projection_weights(w_in)
    w_out_b = _out_proj_weight(w_out)
    w_gate_b, w_up_b, w_down_b = w_gate.astype(BF16), w_up.astype(BF16), w_down.astype(BF16)

    for layer in range(DEPTH):
        lam_vec = jnp.stack([lambda_q1[layer], lambda_k1[layer], lambda_q2[layer], lambda_k2[layer]])
        proj_a, proj_t = _pre_call(layer, x, mod3, gains, w_a, w_t, tables)
        o_diff = _diff_call(layer, proj_a, proj_t, lam_vec, subln_g[layer])
        o_swa = _swa_call(proj_a, proj_t, sinks[layer])
        x = _post_call(layer, x, o_diff, o_swa, mod3, gains, w_out_b, w_gate_b, w_up_b, w_down_b)
    return x
```

```python
import functools
import math

import numpy as np
import jax
import jax.numpy as jnp
from jax import lax
from jax.experimental import pallas as pl
from jax.experimental.pallas import tpu as pltpu

D_MODEL = 1024
BATCH = 16
SEQ = 2048
DEPTH = 2
HEAD_DIM = 64
HALF = HEAD_DIM // 2
DIFF_HEADS = 4
DIFF_WIDTH = 512
SWA_HEADS = 8
SWA_KV_HEADS = 2
SWA_GROUP = SWA_HEADS // SWA_KV_HEADS
SWA_WIDTH = 512
WINDOW = 128
ROPE_THETA = 10000.0
D_FF = 2816
EPS = 1e-6
NEG = -1e30
LOG2E = math.log2(math.e)

LANES = 128

REF_Q_DIFF, REF_K_DIFF, REF_V_DIFF, REF_Q_SWA, REF_K_SWA, REF_V_SWA = 0, 512, 1024, 1536, 2048, 2176
A_K_DIFF, A_K_SWA, A_WIDTH = 0, 512, 640
T_Q_DIFF, T_V_DIFF, T_Q_SWA, T_V_SWA, T_WIDTH = 0, 512, 1024, 1536, 1664
T_ROPE_ROWS = ((T_Q_DIFF, DIFF_WIDTH), (T_Q_SWA, SWA_WIDTH))

VMEM_LIMIT_BYTES = 56 * 1024 * 1024

TM_PRE = 512
TM_POST = 1024
SUB_POST = 512
FF_CHUNKS = ((0, 1536), (1536, 2816))
T_DIFF = 256
DENOM_ROWS = 16
TQ_SWA = 2048
SWA_PIPE_GROUP = 4

F32 = jnp.float32
BF16 = jnp.bfloat16
NT_DIMS = (((1,), (1,)), ((), ()))


def _projection_weights(w_in):
    def section(start, width):
        return w_in[:, :, start:start + width]

    def halves_outer(w, pairs):
        w = w.reshape(DEPTH, D_MODEL, pairs, 2, 2, HALF)
        return jnp.swapaxes(w, 3, 4).reshape(DEPTH, D_MODEL, pairs * LANES)

    score_scale = HEAD_DIM ** -0.5
    q_diff = halves_outer(section(REF_Q_DIFF, DIFF_WIDTH), DIFF_HEADS) * score_scale
    k_diff = halves_outer(section(REF_K_DIFF, DIFF_WIDTH), DIFF_HEADS)
    v_diff = section(REF_V_DIFF, DIFF_WIDTH)
    q_swa = section(REF_Q_SWA, SWA_WIDTH).reshape(DEPTH, D_MODEL, SWA_KV_HEADS, SWA_GROUP, 2, HALF)
    q_swa = q_swa.transpose(0, 1, 3, 4, 2, 5).reshape(DEPTH, D_MODEL, SWA_WIDTH) * score_scale
    k_swa = halves_outer(section(REF_K_SWA, LANES), 1)
    v_swa = section(REF_V_SWA, LANES)
    w_a = jnp.concatenate([k_diff, k_swa], axis=-1).astype(BF16)
    w_t = jnp.swapaxes(jnp.concatenate([q_diff, v_diff, q_swa, v_swa], axis=-1), 1, 2).astype(BF16)
    return w_a, w_t


def _out_proj_weight(w_out):
    diff_rows = w_out[:, 0:DIFF_WIDTH, :]
    swa_rows = w_out[:, DIFF_WIDTH:, :].reshape(DEPTH, SWA_KV_HEADS, SWA_GROUP, HEAD_DIM, D_MODEL)
    swa_rows = jnp.swapaxes(swa_rows, 1, 2).reshape(DEPTH, SWA_WIDTH, D_MODEL)
    return jnp.concatenate([diff_rows, swa_rows], axis=1).astype(BF16)


def _rope_tables():
    pos = jnp.arange(SEQ, dtype=F32)
    inv = ROPE_THETA ** (-jnp.arange(0, HEAD_DIM, 2, dtype=F32) / HEAD_DIM)
    ang = pos[:, None] * inv[None, :]
    cos, sin = jnp.cos(ang), jnp.sin(ang)
    cos4 = jnp.concatenate([cos, cos, cos, cos], axis=-1)
    sin4 = jnp.concatenate([-sin, -sin, sin, sin], axis=-1)
    cos_t = jnp.concatenate([cos, cos], axis=-1).T * LOG2E
    sin_t = jnp.concatenate([sin, sin], axis=-1).T * LOG2E
    return cos4, sin4, cos_t, sin_t


def _rms(x):
    return x * lax.rsqrt(jnp.mean(x * x, axis=-1, keepdims=True) + EPS)


def _ada_kernel(c_ref, w_ref, b_ref, o_ref):
    c = c_ref[...]
    c_act = c / (1.0 + jnp.exp(-c))
    o_ref[0] = jnp.dot(c_act, w_ref[0], preferred_element_type=F32) + b_ref[0]


def _ada_call(c, ada_w, ada_b):
    n_chunk = 6
    return pl.pallas_call(
        _ada_kernel,
        grid=(DEPTH, n_chunk),
        in_specs=[
            pl.BlockSpec((BATCH, D_MODEL), lambda l, k: (0, 0)),
            pl.BlockSpec((1, D_MODEL, D_MODEL), lambda l, k: (l, 0, k)),
            pl.BlockSpec((1, 1, D_MODEL), lambda l, k: (l, 0, k)),
        ],
        out_specs=pl.BlockSpec((1, BATCH, D_MODEL), lambda l, k: (l, 0, k)),
        out_shape=jax.ShapeDtypeStruct((DEPTH, BATCH, 6 * D_MODEL), F32),
        name="ada_mod",
    )(c, ada_w, ada_b.reshape(DEPTH, 1, 6 * D_MODEL))


def _pre_kernel(x_ref, gain_ref, mod_ref, wa_ref, wt_ref, cos_ref, sin_ref, cost_ref, sint_ref,
                oa_ref, ot_ref):
    gain = _row(gain_ref, GAIN_MIX_PRE)
    shift, scale = _row(mod_ref, MOD_SH1), _row(mod_ref, MOD_SC1)

    def modulate(tok):
        x = x_ref[0, tok, :]
        h = (_rms(x) * gain) * (1.0 + scale) + shift
        return h.astype(BF16)

    def project(hb):
        return (jnp.dot(hb, wa_ref[0], preferred_element_type=F32),
                lax.dot_general(wt_ref[0], hb, NT_DIMS, preferred_element_type=F32))

    def rotate_store(blk, tok, proj, proj_t):
        cos = cos_ref[tok, :]
        sin = sin_ref[tok, :]
        for grp in range(A_WIDTH // LANES):
            cols = slice(grp * LANES, (grp + 1) * LANES)
            xg = proj[:, cols]
            oa_ref[0, tok, cols] = (xg * cos + pltpu.roll(xg, 64, 1) * sin).astype(BF16)

        cos_t = cost_ref[:, tok]
        sin_t = sint_ref[:, tok]
        for start, size in T_ROPE_ROWS:
            for grp in range(size // LANES):
                base = start + grp * LANES
                lo = proj_t[base:base + 64]
                hi = proj_t[base + 64:base + LANES]
                ot_ref[0, blk, base:base + 64, :] = (lo * cos_t - hi * sin_t).astype(BF16)
                ot_ref[0, blk, base + 64:base + LANES, :] = (hi * cos_t + lo * sin_t).astype(BF16)
        for start, size in ((T_V_DIFF, DIFF_WIDTH), (T_V_SWA, LANES)):
            ot_ref[0, blk, start:start + size, :] = proj_t[start:start + size].astype(BF16)

    toks = [slice(blk * T_DIFF, (blk + 1) * T_DIFF) for blk in range(TM_PRE // T_DIFF)]
    staged = [modulate(tok) for tok in toks]
    projected = [project(hb) for hb in staged]
    for blk, (tok, (proj, proj_t)) in enumerate(zip(toks, projected)):
        rotate_store(blk, tok, proj, proj_t)


MOD_SH1, MOD_SC1, MOD_GT1, MOD_SH2, MOD_SC2, MOD_GT2 = range(6)
GAIN_MIX_PRE, GAIN_MIX_POST, GAIN_FFN_PRE, GAIN_FFN_POST = range(4)


def _mod_spec(layer):
    return pl.BlockSpec((1, 6, D_MODEL), lambda b, i: (layer * BATCH + b, 0, 0))


def _gain_spec(layer):
    return pl.BlockSpec((1, 4, D_MODEL), lambda b, i: (layer, 0, 0))


def _row(ref, k):
    return ref[0, k:k + 1, :]


def _layer_weight(layer, shape):
    return pl.BlockSpec((1,) + shape, lambda b, i: (layer,) + (0,) * len(shape), pipeline_mode=pl.Buffered(1))


def _pre_call(layer, x, mod3, gains, w_a, w_t, tables):
    cos4, sin4, cos_t, sin_t = tables
    n_t = SEQ // TM_PRE
    n_blk = TM_PRE // T_DIFF
    return pl.pallas_call(
        _pre_kernel,
        grid=(BATCH, n_t),
        in_specs=[
            pl.BlockSpec((1, TM_PRE, D_MODEL), lambda b, i: (b, i, 0)),
            _gain_spec(layer),
            _mod_spec(layer),
            _layer_weight(layer, (D_MODEL, A_WIDTH)),
            _layer_weight(layer, (T_WIDTH, D_MODEL)),
            pl.BlockSpec((TM_PRE, LANES), lambda b, i: (i, 0)),
            pl.BlockSpec((TM_PRE, LANES), lambda b, i: (i, 0)),
            pl.BlockSpec((HEAD_DIM, TM_PRE), lambda b, i: (0, i)),
            pl.BlockSpec((HEAD_DIM, TM_PRE), lambda b, i: (0, i)),
        ],
        out_specs=[
            pl.BlockSpec((1, TM_PRE, A_WIDTH), lambda b, i: (b, i, 0)),
            pl.BlockSpec((1, n_blk, T_WIDTH, T_DIFF), lambda b, i: (b, i, 0, 0)),
        ],
        out_shape=[
            jax.ShapeDtypeStruct((BATCH, SEQ, A_WIDTH), BF16),
            jax.ShapeDtypeStruct((BATCH, SEQ // T_DIFF, T_WIDTH, T_DIFF), BF16),
        ],
        compiler_params=pltpu.CompilerParams(
            dimension_semantics=("arbitrary", "arbitrary"),
            vmem_limit_bytes=VMEM_LIMIT_BYTES),
        name="pre_mixer",
    )(x, gains, mod3, w_a, w_t, cos4, sin4, cos_t, sin_t)


def _diff_kernel(lam_ref, g_ref, qt_ref, k_ref, vt_ref, o_ref,
                 qbd_ref, m_ref, acc_ref, s0_ref, s1_ref, max0_ref, max1_ref, *, lam_init):
    pair_idx = pl.program_id(1)
    t = T_DIFF
    feat = lax.broadcasted_iota(jnp.int32, (LANES, 1), 0)
    first_map = ((feat // HALF) % 2) == 0

    chains = [(sub, h) for sub in range(2) for h in range(DIFF_HEADS)]
    sub_chains = lambda sub: [c for c, (sb, _) in enumerate(chains) if sb == sub]
    all_chains = list(range(len(chains)))

    for c, (sub, h) in enumerate(chains):
        qh = qt_ref[0, sub, h * LANES:(h + 1) * LANES, :]
        zero = jnp.zeros_like(qh)
        qbd_ref[c, :, 0:t] = jnp.where(first_map, qh, zero)
        qbd_ref[c, :, t:2 * t] = jnp.where(first_map, zero, qh)
        m_ref[c] = jnp.full((1, 2 * t), NEG, F32)
        acc_ref[c] = jnp.zeros((LANES + DENOM_ROWS, 2 * t), F32)
    ones_rows = jnp.ones((DENOM_ROWS, t), BF16)

    def scores(j, c, s_ref, max_ref):
        h = chains[c][1]
        start = pl.multiple_of(j * t, t)
        kh = k_ref[0, pl.ds(start, t), h * LANES:(h + 1) * LANES]
        s = jnp.dot(kh, qbd_ref[c], preferred_element_type=F32)
        s_ref[c] = s
        max_ref[c] = jnp.max(s, axis=0, keepdims=True)

    def accumulate(j, c, s_ref, max_ref, diagonal):
        h = chains[c][1]
        vth = vt_ref[0, j, h * LANES:(h + 1) * LANES, :]
        s = s_ref[c]
        if diagonal:
            key = lax.broadcasted_iota(jnp.int32, (t, 2 * t), 0)
            qry = lax.broadcasted_iota(jnp.int32, (t, 2 * t), 1) & (t - 1)
            s = jnp.where(key <= qry, s, NEG)
            m_blk = jnp.max(s, axis=0, keepdims=True)
        else:
            m_blk = max_ref[c]
        m_old = m_ref[c]
        m_new = jnp.maximum(m_old, m_blk)
        p = jnp.exp2(s - m_new)
        alpha = jnp.exp2(m_old - m_new)
        v_aug = jnp.concatenate([vth, ones_rows], axis=0)
        acc_ref[c] = alpha * acc_ref[c] + jnp.dot(v_aug, p.astype(BF16), preferred_element_type=F32)
        m_ref[c] = m_new

    for c in all_chains:
        scores(0, c, s0_ref, max0_ref)

    def two_blocks(step, carry):
        j = 2 * step
        for c in all_chains:
            scores(j + 1, c, s1_ref, max1_ref)
            accumulate(j, c, s0_ref, max0_ref, False)
        for c in all_chains:
            scores(j + 2, c, s0_ref, max0_ref)
            accumulate(j + 1, c, s1_ref, max1_ref, False)
        return carry

    lax.fori_loop(0, pair_idx, two_blocks, 0)

    lam_vec = lam_ref[...]
    dot1 = jnp.sum(lam_vec[0:1] * lam_vec[1:2], axis=-1, keepdims=True)
    dot2 = jnp.sum(lam_vec[2:3] * lam_vec[3:4], axis=-1, keepdims=True)
    lam = jnp.exp(dot1) - jnp.exp(dot2) + lam_init
    g = g_ref[...]

    def finalize(c):
        sub, h = chains[c]
        denom = acc_ref[c, LANES:LANES + 1, :]
        o_all = acc_ref[c, 0:LANES, :] * (1.0 / denom)
        o = o_all[:, 0:t] - lam * o_all[:, t:2 * t]
        o = o * lax.rsqrt(jnp.mean(o * o, axis=0, keepdims=True) + EPS)
        o = (o * g) * (1.0 - lam_init)
        o_ref[0, sub * t:(sub + 1) * t, h * LANES:(h + 1) * LANES] = o.T.astype(BF16)

    diag0 = 2 * pair_idx
    for c0, c1 in zip(sub_chains(0), sub_chains(1)):
        scores(diag0 + 1, c1, s1_ref, max1_ref)
        accumulate(diag0, c0, s0_ref, max0_ref, True)
        accumulate(diag0, c1, s0_ref, max0_ref, False)
    for c0, c1 in zip(sub_chains(0), sub_chains(1)):
        accumulate(diag0 + 1, c1, s1_ref, max1_ref, True)
        finalize(c0)
    for c1 in sub_chains(1):
        finalize(c1)


def _diff_call(layer, proj_a, proj_t, lam_vec, subln_g):
    lam_init = 0.8 - 0.6 * math.exp(-0.3 * layer)
    n_blocks = SEQ // T_DIFF
    n_chains = 2 * DIFF_HEADS
    kernel = functools.partial(_diff_kernel, lam_init=lam_init)
    return pl.pallas_call(
        kernel,
        grid=(BATCH, n_blocks // 2),
        in_specs=[
            pl.BlockSpec((4, HEAD_DIM), lambda b, i: (0, 0)),
            pl.BlockSpec((LANES, 1), lambda b, i: (0, 0)),
            pl.BlockSpec((1, 2, DIFF_WIDTH, T_DIFF), lambda b, i: (b, i, T_Q_DIFF // DIFF_WIDTH, 0)),
            pl.BlockSpec((1, SEQ, DIFF_WIDTH), lambda b, i: (b, 0, A_K_DIFF // DIFF_WIDTH)),
            pl.BlockSpec((1, n_blocks, DIFF_WIDTH, T_DIFF), lambda b, i: (b, 0, T_V_DIFF // DIFF_WIDTH, 0)),
        ],
        out_specs=pl.BlockSpec((1, 2 * T_DIFF, DIFF_WIDTH), lambda b, i: (b, i, 0)),
        out_shape=jax.ShapeDtypeStruct((BATCH, SEQ, DIFF_WIDTH), BF16),
        scratch_shapes=[
            pltpu.VMEM((n_chains, LANES, 2 * T_DIFF), BF16),
            pltpu.VMEM((n_chains, 1, 2 * T_DIFF), F32),
            pltpu.VMEM((n_chains, LANES + DENOM_ROWS, 2 * T_DIFF), F32),
            pltpu.VMEM((n_chains, T_DIFF, 2 * T_DIFF), F32),
            pltpu.VMEM((n_chains, T_DIFF, 2 * T_DIFF), F32),
            pltpu.VMEM((n_chains, 1, 2 * T_DIFF), F32),
            pltpu.VMEM((n_chains, 1, 2 * T_DIFF), F32),
        ],
        compiler_params=pltpu.CompilerParams(
            dimension_semantics=("arbitrary", "arbitrary"),
            vmem_limit_bytes=VMEM_LIMIT_BYTES),
        name="diff_attn",
    )(lam_vec, subln_g.reshape(LANES, 1), proj_t, proj_a, proj_t)


def _swa_kernel(sink_ref, qt_ref, kprev_ref, kcur_ref, vtprev_ref, vtcur_ref, o_ref,
                s_ref, pv_ref, m_ref, bias_ref):
    i = pl.program_id(1)
    w = WINDOW
    n_sub = TQ_SWA // w
    feat = lax.broadcasted_iota(jnp.int32, (LANES, 1), 0)
    kv0_feat = ((feat // HALF) % 2) == 0
    col = lax.broadcasted_iota(jnp.int32, (1, 2 * w), 1)
    first_cols = col < w

    @pl.when(i == 0)
    def _():
        key = lax.broadcasted_iota(jnp.int32, (2 * w, 2 * w), 0)
        qry = lax.broadcasted_iota(jnp.int32, (2 * w, 2 * w), 1) & (w - 1)
        in_window = (key > qry) & (key <= qry + w)
        bias_ref[0] = jnp.where(in_window, 0.0, NEG)
        bias_ref[1] = jnp.where(in_window & (key >= w), 0.0, NEG)
    ones_rows = jnp.ones((DENOM_ROWS, 2 * w), BF16)

    def band(n):
        tile, half = divmod(n, T_DIFF // w)
        if n == 0:
            k_band = jnp.concatenate([kprev_ref[0], kcur_ref[0, 0:w, :]], axis=0)
            vt_band = jnp.concatenate([vtprev_ref[0, 0, :, w:2 * w], vtcur_ref[0, 0, :, 0:w]], axis=1)
            return k_band, vt_band, bias_ref[jnp.where(i == 0, 1, 0)]
        k_band = kcur_ref[0, (n - 1) * w:(n + 1) * w, :]
        if half == 1:
            vt_band = vtcur_ref[0, tile]
        else:
            vt_band = jnp.concatenate([vtcur_ref[0, tile - 1, :, w:2 * w], vtcur_ref[0, tile, :, 0:w]], axis=1)
        return k_band, vt_band, bias_ref[0]

    def scores(n, g):
        tile, half = divmod(n, T_DIFF // w)
        k_band, _, bias = band(n)
        qg = qt_ref[0, tile, g * LANES:(g + 1) * LANES, half * w:(half + 1) * w]
        zero = jnp.zeros_like(qg)
        qbd = jnp.concatenate([jnp.where(kv0_feat, qg, zero), jnp.where(kv0_feat, zero, qg)], axis=1)
        s = jnp.dot(k_band, qbd, preferred_element_type=F32)
        s_ref[n * SWA_GROUP + g] = s + bias

    def sink_row(g):
        return jnp.where(first_cols, sink_ref[g], sink_ref[SWA_GROUP + g]) * LOG2E

    def attend(n, g):
        c = n * SWA_GROUP + g
        _, vt_band, _ = band(n)
        v_aug = jnp.concatenate([vt_band, ones_rows], axis=0)
        s = s_ref[c]
        m = jnp.maximum(jnp.max(s, axis=0, keepdims=True), sink_row(g))
        p = jnp.exp2(s - m)
        pv_ref[c] = jnp.dot(v_aug, p.astype(BF16), preferred_element_type=F32)
        m_ref[c] = m

    def finish(n, g):
        c = n * SWA_GROUP + g
        pv = pv_ref[c]
        inv = 1.0 / (pv[LANES:LANES + 1, :] + jnp.exp2(sink_row(g) - m_ref[c]))
        o_pair = jnp.concatenate([pv[0:HEAD_DIM, 0:w] * inv[:, 0:w],
                                  pv[HEAD_DIM:LANES, w:2 * w] * inv[:, w:2 * w]], axis=0)
        o_ref[0, n * w:(n + 1) * w, g * LANES:(g + 1) * LANES] = o_pair.T.astype(BF16)

    order = [(n, g) for n in range(n_sub) for g in range(SWA_GROUP)]
    stages = (scores, attend, finish)
    group = SWA_PIPE_GROUP
    groups = [order[k:k + group] for k in range(0, len(order), group)]
    for tick in range(len(groups) + len(stages) - 1):
        for lag, stage in enumerate(stages):
            if 0 <= tick - lag < len(groups):
                for chain in groups[tick - lag]:
                    stage(*chain)


def _swa_call(proj_a, proj_t, sinks):
    n_q = SEQ // TQ_SWA
    tiles = TQ_SWA // T_DIFF
    subs = TQ_SWA // WINDOW
    return pl.pallas_call(
        _swa_kernel,
        grid=(BATCH, n_q),
        in_specs=[
            pl.BlockSpec(memory_space=pltpu.SMEM),
            pl.BlockSpec((1, tiles, SWA_WIDTH, T_DIFF), lambda b, i: (b, i, T_Q_SWA // SWA_WIDTH, 0)),
            pl.BlockSpec((1, WINDOW, LANES), lambda b, i: (b, jnp.maximum(i * subs - 1, 0), A_K_SWA // LANES)),
            pl.BlockSpec((1, TQ_SWA, LANES), lambda b, i: (b, i, A_K_SWA // LANES)),
            pl.BlockSpec((1, 1, LANES, T_DIFF), lambda b, i: (b, jnp.maximum(i * tiles - 1, 0), T_V_SWA // LANES, 0)),
            pl.BlockSpec((1, tiles, LANES, T_DIFF), lambda b, i: (b, i, T_V_SWA // LANES, 0)),
        ],
        out_specs=pl.BlockSpec((1, TQ_SWA, SWA_WIDTH), lambda b, i: (b, i, 0)),
        out_shape=jax.ShapeDtypeStruct((BATCH, SEQ, SWA_WIDTH), BF16),
        scratch_shapes=[
            pltpu.VMEM((subs * SWA_GROUP, 2 * WINDOW, 2 * WINDOW), F32),
            pltpu.VMEM((subs * SWA_GROUP, LANES + DENOM_ROWS, 2 * WINDOW), F32),
            pltpu.VMEM((subs * SWA_GROUP, 1, 2 * WINDOW), F32),
            pltpu.VMEM((2, 2 * WINDOW, 2 * WINDOW), F32),
        ],
        compiler_params=pltpu.CompilerParams(
            dimension_semantics=("arbitrary", "arbitrary"),
            vmem_limit_bytes=VMEM_LIMIT_BYTES),
        name="swa_attn",
    )(sinks, proj_t, proj_a, proj_a, proj_t, proj_t)


def _post_kernel(x_ref, od_ref, os_ref, gain_ref, mod_ref, wo_ref, wg_ref, wu_ref, wd_ref, o_ref):
    g_mix_post, g_ffn_pre, g_ffn_post = (_row(gain_ref, k) for k in (GAIN_MIX_POST, GAIN_FFN_PRE, GAIN_FFN_POST))
    gate1, shift2, scale2, gate2 = (_row(mod_ref, k) for k in (MOD_GT1, MOD_SH2, MOD_SC2, MOD_GT2))

    def mixer_residual(rows):
        x = x_ref[0, rows, :]
        y = jnp.dot(od_ref[0, rows, :], wo_ref[0, 0:DIFF_WIDTH, :], preferred_element_type=F32)
        y = y + jnp.dot(os_ref[0, rows, :], wo_ref[0, DIFF_WIDTH:D_MODEL, :], preferred_element_type=F32)
        x1 = x + gate1 * (_rms(y) * g_mix_post)
        h = (_rms(x1) * g_ffn_pre) * (1.0 + scale2) + shift2
        return x1, h.astype(BF16)

    def ffn_residual(x1, hb):
        y2 = None
        for lo, hi in FF_CHUNKS:
            gate = jnp.dot(hb, wg_ref[0, :, lo:hi], preferred_element_type=F32)
            up = jnp.dot(hb, wu_ref[0, :, lo:hi], preferred_element_type=F32)
            act = ((gate / (1.0 + jnp.exp(-gate))) * up).astype(BF16)
            part = jnp.dot(act, wd_ref[0, lo:hi, :], preferred_element_type=F32)
            y2 = part if y2 is None else y2 + part
        return x1 + gate2 * (_rms(y2) * g_ffn_post)

    subs = [slice(k * SUB_POST, (k + 1) * SUB_POST) for k in range(TM_POST // SUB_POST)]
    staged = [mixer_residual(rows) for rows in subs]
    for rows, (x1, hb) in zip(subs, staged):
        o_ref[0, rows, :] = ffn_residual(x1, hb)


def _post_call(layer, x, o_diff, o_swa, mod3, gains, w_out_b, w_gate_b, w_up_b, w_down_b):
    n_t = SEQ // TM_POST
    return pl.pallas_call(
        _post_kernel,
        grid=(BATCH, n_t),
        in_specs=[
            pl.BlockSpec((1, TM_POST, D_MODEL), lambda b, i: (b, i, 0)),
            pl.BlockSpec((1, TM_POST, DIFF_WIDTH), lambda b, i: (b, i, 0)),
            pl.BlockSpec((1, TM_POST, SWA_WIDTH), lambda b, i: (b, i, 0)),
            _gain_spec(layer),
            _mod_spec(layer),
            _layer_weight(layer, (D_MODEL, D_MODEL)),
            _layer_weight(layer, (D_MODEL, D_FF)),
            _layer_weight(layer, (D_MODEL, D_FF)),
            _layer_weight(layer, (D_FF, D_MODEL)),
        ],
        out_specs=pl.BlockSpec((1, TM_POST, D_MODEL), lambda b, i: (b, i, 0)),
        out_shape=jax.ShapeDtypeStruct((BATCH, SEQ, D_MODEL), F32),
        compiler_params=pltpu.CompilerParams(
            dimension_semantics=("arbitrary", "arbitrary"),
            vmem_limit_bytes=VMEM_LIMIT_BYTES),
        name="post_ffn",
    )(x, o_diff, o_swa, gains, mod3, w_out_b, w_gate_b, w_up_b, w_down_b)


def kernel(x, c, ada_w, ada_b, g_mix_pre, g_mix_post, g_ffn_pre, g_ffn_post, w_in, lambda_q1, lambda_k1,
           lambda_q2, lambda_k2, subln_g, sinks, w_out, w_gate, w_up, w_down):
    tables = _rope_tables()

    mod = _ada_call(c, ada_w, ada_b)
    mod3 = mod.reshape(DEPTH * BATCH, 6, D_MODEL)
    gains = jnp.stack([g_mix_pre, g_mix_post, g_ffn_pre, g_ffn_post], axis=1)

    w_a, w_t = _projection_weights(w_in)
    w_out_b = _out_proj_weight(w_out)
    w_gate_b, w_up_b, w_down_b = w_gate.astype(BF16), w_up.astype(BF16), w_down.astype(BF16)

    for layer in range(DEPTH):
        lam_vec = jnp.stack([lambda_q1[layer], lambda_k1[layer], lambda_q2[layer], lambda_k2[layer]])
        proj_a, proj_t = _pre_call(layer, x, mod3, gains, w_a, w_t, tables)
        o_diff = _diff_call(layer, proj_a, proj_t, lam_vec, subln_g[layer])
        o_swa = _swa_call(proj_a, proj_t, sinks[layer])
        x = _post_call(layer, x, o_diff, o_swa, mod3, gains, w_out_b, w_gate_b, w_up_b, w_down_b)
    return x
```

```python
import functools
import math

import numpy as np
import jax
import jax.numpy as jnp
from jax import lax
from jax.experimental import pallas as pl
from jax.experimental.pallas import tpu as pltpu

D_MODEL = 1024
BATCH = 16
SEQ = 2048
DEPTH = 2
HEAD_DIM = 64
HALF = HEAD_DIM // 2
DIFF_HEADS = 4
DIFF_WIDTH = 512
SWA_HEADS = 8
SWA_KV_HEADS = 2
SWA_GROUP = SWA_HEADS // SWA_KV_HEADS
SWA_WIDTH = 512
WINDOW = 128
ROPE_THETA = 10000.0
D_FF = 2816
EPS = 1e-6
NEG = -1e30
LOG2E = math.log2(math.e)

LANES = 128

REF_Q_DIFF, REF_K_DIFF, REF_V_DIFF, REF_Q_SWA, REF_K_SWA, REF_V_SWA = 0, 512, 1024, 1536, 2048, 2176
A_K_DIFF, A_K_SWA, A_WIDTH = 0, 512, 640
T_Q_DIFF, T_V_DIFF, T_Q_SWA, T_V_SWA, T_WIDTH = 0, 512, 1024, 1536, 1664
T_ROPE_ROWS = ((T_Q_DIFF, DIFF_WIDTH), (T_Q_SWA, SWA_WIDTH))

VMEM_LIMIT_BYTES = 56 * 1024 * 1024

TM_PRE = 512
TM_POST = 1024
SUB_POST = 256
T_DIFF = 256
DENOM_ROWS = 16
TQ_SWA = 2048
SWA_PIPE_GROUP = 4

F32 = jnp.float32
BF16 = jnp.bfloat16
NT_DIMS = (((1,), (1,)), ((), ()))


def _projection_weights(w_in):
    def section(start, width):
        return w_in[:, :, start:start + width]

    def halves_outer(w, pairs):
        w = w.reshape(DEPTH, D_MODEL, pairs, 2, 2, HALF)
        return jnp.swapaxes(w, 3, 4).reshape(DEPTH, D_MODEL, pairs * LANES)

    score_scale = HEAD_DIM ** -0.5
    q_diff = halves_outer(section(REF_Q_DIFF, DIFF_WIDTH), DIFF_HEADS) * score_scale
    k_diff = halves_outer(section(REF_K_DIFF, DIFF_WIDTH), DIFF_HEADS)
    v_diff = section(REF_V_DIFF, DIFF_WIDTH)
    q_swa = section(REF_Q_SWA, SWA_WIDTH).reshape(DEPTH, D_MODEL, SWA_KV_HEADS, SWA_GROUP, 2, HALF)
    q_swa = q_swa.transpose(0, 1, 3, 4, 2, 5).reshape(DEPTH, D_MODEL, SWA_WIDTH) * score_scale
    k_swa = halves_outer(section(REF_K_SWA, LANES), 1)
    v_swa = section(REF_V_SWA, LANES)
    w_a = jnp.concatenate([k_diff, k_swa], axis=-1).astype(BF16)
    w_t = jnp.swapaxes(jnp.concatenate([q_diff, v_diff, q_swa, v_swa], axis=-1), 1, 2).astype(BF16)
    return w_a, w_t


def _out_proj_weight(w_out):
    diff_rows = w_out[:, 0:DIFF_WIDTH, :]
    swa_rows = w_out[:, DIFF_WIDTH:, :].reshape(DEPTH, SWA_KV_HEADS, SWA_GROUP, HEAD_DIM, D_MODEL)
    swa_rows = jnp.swapaxes(swa_rows, 1, 2).reshape(DEPTH, SWA_WIDTH, D_MODEL)
    return jnp.concatenate([diff_rows, swa_rows], axis=1).astype(BF16)


def _rope_tables():
    pos = jnp.arange(SEQ, dtype=F32)
    inv = ROPE_THETA ** (-jnp.arange(0, HEAD_DIM, 2, dtype=F32) / HEAD_DIM)
    ang = pos[:, None] * inv[None, :]
    cos, sin = jnp.cos(ang), jnp.sin(ang)
    cos4 = jnp.concatenate([cos, cos, cos, cos], axis=-1)
    sin4 = jnp.concatenate([-sin, -sin, sin, sin], axis=-1)
    cos_t = jnp.concatenate([cos, cos], axis=-1).T * LOG2E
    sin_t = jnp.concatenate([sin, sin], axis=-1).T * LOG2E
    return cos4, sin4, cos_t, sin_t


def _rms(x):
    return x * lax.rsqrt(jnp.mean(x * x, axis=-1, keepdims=True) + EPS)


def _ada_kernel(c_ref, w_ref, b_ref, o_ref):
    c = c_ref[...]
    c_act = c / (1.0 + jnp.exp(-c))
    o_ref[0] = jnp.dot(c_act, w_ref[0], preferred_element_type=F32) + b_ref[0]


def _ada_call(c, ada_w, ada_b):
    n_chunk = 6
    return pl.pallas_call(
        _ada_kernel,
        grid=(DEPTH, n_chunk),
        in_specs=[
            pl.BlockSpec((BATCH, D_MODEL), lambda l, k: (0, 0)),
            pl.BlockSpec((1, D_MODEL, D_MODEL), lambda l, k: (l, 0, k)),
            pl.BlockSpec((1, 1, D_MODEL), lambda l, k: (l, 0, k)),
        ],
        out_specs=pl.BlockSpec((1, BATCH, D_MODEL), lambda l, k: (l, 0, k)),
        out_shape=jax.ShapeDtypeStruct((DEPTH, BATCH, 6 * D_MODEL), F32),
        name="ada_mod",
    )(c, ada_w, ada_b.reshape(DEPTH, 1, 6 * D_MODEL))


def _pre_kernel(x_ref, gain_ref, mod_ref, wa_ref, wt_ref, cos_ref, sin_ref, cost_ref, sint_ref,
                oa_ref, ot_ref):
    gain = _row(gain_ref, GAIN_MIX_PRE)
    shift, scale = _row(mod_ref, MOD_SH1), _row(mod_ref, MOD_SC1)

    def modulate(tok):
        x = x_ref[0, tok, :]
        h = (_rms(x) * gain) * (1.0 + scale) + shift
        return h.astype(BF16)

    def project(hb):
        return (jnp.dot(hb, wa_ref[0], preferred_element_type=F32),
                lax.dot_general(wt_ref[0], hb, NT_DIMS, preferred_element_type=F32))

    def rotate_store(blk, tok, proj, proj_t):
        cos = cos_ref[tok, :]
        sin = sin_ref[tok, :]
        for grp in range(A_WIDTH // LANES):
            cols = slice(grp * LANES, (grp + 1) * LANES)
            xg = proj[:, cols]
            oa_ref[0, tok, cols] = (xg * cos + pltpu.roll(xg, 64, 1) * sin).astype(BF16)

        cos_t = cost_ref[:, tok]
        sin_t = sint_ref[:, tok]
        for start, size in T_ROPE_ROWS:
            for grp in range(size // LANES):
                base = start + grp * LANES
                lo = proj_t[base:base + 64]
                hi = proj_t[base + 64:base + LANES]
                ot_ref[0, blk, base:base + 64, :] = (lo * cos_t - hi * sin_t).astype(BF16)
                ot_ref[0, blk, base + 64:base + LANES, :] = (hi * cos_t + lo * sin_t).astype(BF16)
        for start, size in ((T_V_DIFF, DIFF_WIDTH), (T_V_SWA, LANES)):
            ot_ref[0, blk, start:start + size, :] = proj_t[start:start + size].astype(BF16)

    toks = [slice(blk * T_DIFF, (blk + 1) * T_DIFF) for blk in range(TM_PRE // T_DIFF)]
    staged = [modulate(tok) for tok in toks]
    projected = [project(hb) for hb in staged]
    for blk, (tok, (proj, proj_t)) in enumerate(zip(toks, projected)):
        rotate_store(blk, tok, proj, proj_t)


MOD_SH1, MOD_SC1, MOD_GT1, MOD_SH2, MOD_SC2, MOD_GT2 = range(6)
GAIN_MIX_PRE, GAIN_MIX_POST, GAIN_FFN_PRE, GAIN_FFN_POST = range(4)


def _mod_spec(layer):
    return pl.BlockSpec((1, 6, D_MODEL), lambda b, i: (layer * BATCH + b, 0, 0))


def _gain_spec(layer):
    return pl.BlockSpec((1, 4, D_MODEL), lambda b, i: (layer, 0, 0))


def _row(ref, k):
    return ref[0, k:k + 1, :]


def _layer_weight(layer, shape):
    return pl.BlockSpec((1,) + shape, lambda b, i: (layer,) + (0,) * len(shape), pipeline_mode=pl.Buffered(1))


def _pre_call(layer, x, mod3, gains, w_a, w_t, tables):
    cos4, sin4, cos_t, sin_t = tables
    n_t = SEQ // TM_PRE
    n_blk = TM_PRE // T_DIFF
    return pl.pallas_call(
        _pre_kernel,
        grid=(BATCH, n_t),
        in_specs=[
            pl.BlockSpec((1, TM_PRE, D_MODEL), lambda b, i: (b, i, 0)),
            _gain_spec(layer),
            _mod_spec(layer),
            _layer_weight(layer, (D_MODEL, A_WIDTH)),
            _layer_weight(layer, (T_WIDTH, D_MODEL)),
            pl.BlockSpec((TM_PRE, LANES), lambda b, i: (i, 0)),
            pl.BlockSpec((TM_PRE, LANES), lambda b, i: (i, 0)),
            pl.BlockSpec((HEAD_DIM, TM_PRE), lambda b, i: (0, i)),
            pl.BlockSpec((HEAD_DIM, TM_PRE), lambda b, i: (0, i)),
        ],
        out_specs=[
            pl.BlockSpec((1, TM_PRE, A_WIDTH), lambda b, i: (b, i, 0)),
            pl.BlockSpec((1, n_blk, T_WIDTH, T_DIFF), lambda b, i: (b, i, 0, 0)),
        ],
        out_shape=[
            jax.ShapeDtypeStruct((BATCH, SEQ, A_WIDTH), BF16),
            jax.ShapeDtypeStruct((BATCH, SEQ // T_DIFF, T_WIDTH, T_DIFF), BF16),
        ],
        compiler_params=pltpu.CompilerParams(
            dimension_semantics=("arbitrary", "arbitrary"),
            vmem_limit_bytes=VMEM_LIMIT_BYTES),
        name="pre_mixer",
    )(x, gains, mod3, w_a, w_t, cos4, sin4, cos_t, sin_t)


def _diff_kernel(lam_ref, g_ref, qt_ref, k_ref, vt_ref, o_ref,
                 qbd_ref, m_ref, acc_ref, s0_ref, s1_ref, max0_ref, max1_ref, *, lam_init):
    t = T_DIFF
    feat = lax.broadcasted_iota(jnp.int32, (LANES, 1), 0)
    first_map = ((feat // HALF) % 2) == 0

    chains = [(sub, h) for sub in range(2) for h in range(DIFF_HEADS)]
    sub_chains = lambda sub: [c for c, (sb, _) in enumerate(chains) if sb == sub]
    all_chains = list(range(len(chains)))
    ones_rows = jnp.ones((DENOM_ROWS, t), BF16)

    lam_vec = lam_ref[...]
    dot1 = jnp.sum(lam_vec[0:1] * lam_vec[1:2], axis=-1, keepdims=True)
    dot2 = jnp.sum(lam_vec[2:3] * lam_vec[3:4], axis=-1, keepdims=True)
    lam = jnp.exp(dot1) - jnp.exp(dot2) + lam_init
    g = g_ref[...]

    def scores(j, c, s_ref, max_ref):
        h = chains[c][1]
        start = pl.multiple_of(j * t, t)
        kh = k_ref[0, pl.ds(start, t), h * LANES:(h + 1) * LANES]
        s = jnp.dot(kh, qbd_ref[c], preferred_element_type=F32)
        s_ref[c] = s
        max_ref[c] = jnp.max(s, axis=0, keepdims=True)

    def accumulate(j, c, s_ref, max_ref, diagonal):
        h = chains[c][1]
        vth = vt_ref[0, j, h * LANES:(h + 1) * LANES, :]
        s = s_ref[c]
        if diagonal:
            key = lax.broadcasted_iota(jnp.int32, (t, 2 * t), 0)
            qry = lax.broadcasted_iota(jnp.int32, (t, 2 * t), 1) & (t - 1)
            s = jnp.where(key <= qry, s, NEG)
            m_blk = jnp.max(s, axis=0, keepdims=True)
        else:
            m_blk = max_ref[c]
        m_old = m_ref[c]
        m_new = jnp.maximum(m_old, m_blk)
        p = jnp.exp2(s - m_new)
        alpha = jnp.exp2(m_old - m_new)
        v_aug = jnp.concatenate([vth, ones_rows], axis=0)
        acc_ref[c] = alpha * acc_ref[c] + jnp.dot(v_aug, p.astype(BF16), preferred_element_type=F32)
        m_ref[c] = m_new

    def sweep_pair(pair_idx, carry):
        diag0 = 2 * pair_idx
        for c, (sub, h) in enumerate(chains):
            qh = qt_ref[0, diag0 + sub, h * LANES:(h + 1) * LANES, :]
            zero = jnp.zeros_like(qh)
            qbd_ref[c, :, 0:t] = jnp.where(first_map, qh, zero)
            qbd_ref[c, :, t:2 * t] = jnp.where(first_map, zero, qh)
            m_ref[c] = jnp.full((1, 2 * t), NEG, F32)
            acc_ref[c] = jnp.zeros((LANES + DENOM_ROWS, 2 * t), F32)

        for c in all_chains:
            scores(0, c, s0_ref, max0_ref)

        def two_blocks(step, inner):
            j = 2 * step
            for c in all_chains:
                scores(j + 1, c, s1_ref, max1_ref)
                accumulate(j, c, s0_ref, max0_ref, False)
            for c in all_chains:
                scores(j + 2, c, s0_ref, max0_ref)
                accumulate(j + 1, c, s1_ref, max1_ref, False)
            return inner

        lax.fori_loop(0, pair_idx, two_blocks, 0)

        def finalize(c):
            sub, h = chains[c]
            denom = acc_ref[c, LANES:LANES + 1, :]
            o_all = acc_ref[c, 0:LANES, :] * (1.0 / denom)
            o = o_all[:, 0:t] - lam * o_all[:, t:2 * t]
            o = o * lax.rsqrt(jnp.mean(o * o, axis=0, keepdims=True) + EPS)
            o = (o * g) * (1.0 - lam_init)
            rows = pl.ds(pl.multiple_of((diag0 + sub) * t, t), t)
            o_ref[0, rows, h * LANES:(h + 1) * LANES] = o.T.astype(BF16)

        for c0, c1 in zip(sub_chains(0), sub_chains(1)):
            scores(diag0 + 1, c1, s1_ref, max1_ref)
            accumulate(diag0, c0, s0_ref, max0_ref, True)
            accumulate(diag0, c1, s0_ref, max0_ref, False)
        for c0, c1 in zip(sub_chains(0), sub_chains(1)):
            accumulate(diag0 + 1, c1, s1_ref, max1_ref, True)
            finalize(c0)
        for c1 in sub_chains(1):
            finalize(c1)
        return carry

    lax.fori_loop(0, SEQ // (2 * t), sweep_pair, 0)


def _diff_call(layer, proj_a, proj_t, lam_vec, subln_g):
    lam_init = 0.8 - 0.6 * math.exp(-0.3 * layer)
    n_blocks = SEQ // T_DIFF
    n_chains = 2 * DIFF_HEADS
    kernel = functools.partial(_diff_kernel, lam_init=lam_init)
    return pl.pallas_call(
        kernel,
        grid=(BATCH,),
        in_specs=[
            pl.BlockSpec((4, HEAD_DIM), lambda b: (0, 0)),
            pl.BlockSpec((LANES, 1), lambda b: (0, 0)),
            pl.BlockSpec((1, n_blocks, DIFF_WIDTH, T_DIFF), lambda b: (b, 0, T_Q_DIFF // DIFF_WIDTH, 0)),
            pl.BlockSpec((1, SEQ, DIFF_WIDTH), lambda b: (b, 0, A_K_DIFF // DIFF_WIDTH)),
            pl.BlockSpec((1, n_blocks, DIFF_WIDTH, T_DIFF), lambda b: (b, 0, T_V_DIFF // DIFF_WIDTH, 0)),
        ],
        out_specs=pl.BlockSpec((1, SEQ, DIFF_WIDTH), lambda b: (b, 0, 0)),
        out_shape=jax.ShapeDtypeStruct((BATCH, SEQ, DIFF_WIDTH), BF16),
        scratch_shapes=[
            pltpu.VMEM((n_chains, LANES, 2 * T_DIFF), BF16),
            pltpu.VMEM((n_chains, 1, 2 * T_DIFF), F32),
            pltpu.VMEM((n_chains, LANES + DENOM_ROWS, 2 * T_DIFF), F32),
            pltpu.VMEM((n_chains, T_DIFF, 2 * T_DIFF), F32),
            pltpu.VMEM((n_chains, T_DIFF, 2 * T_DIFF), F32),
            pltpu.VMEM((n_chains, 1, 2 * T_DIFF), F32),
            pltpu.VMEM((n_chains, 1, 2 * T_DIFF), F32),
        ],
        compiler_params=pltpu.CompilerParams(
            dimension_semantics=("arbitrary",),
            vmem_limit_bytes=VMEM_LIMIT_BYTES),
        name="diff_attn",
    )(lam_vec, subln_g.reshape(LANES, 1), proj_t, proj_a, proj_t)


def _swa_kernel(sink_ref, qt_ref, kprev_ref, kcur_ref, vtprev_ref, vtcur_ref, o_ref,
                s_ref, pv_ref, m_ref, bias_ref):
    i = pl.program_id(1)
    w = WINDOW
    n_sub = TQ_SWA // w
    feat = lax.broadcasted_iota(jnp.int32, (LANES, 1), 0)
    kv0_feat = ((feat // HALF) % 2) == 0
    col = lax.broadcasted_iota(jnp.int32, (1, 2 * w), 1)
    first_cols = col < w

    @pl.when(i == 0)
    def _():
        key = lax.broadcasted_iota(jnp.int32, (2 * w, 2 * w), 0)
        qry = lax.broadcasted_iota(jnp.int32, (2 * w, 2 * w), 1) & (w - 1)
        in_window = (key > qry) & (key <= qry + w)
        bias_ref[0] = jnp.where(in_window, 0.0, NEG)
        bias_ref[1] = jnp.where(in_window & (key >= w), 0.0, NEG)
    ones_rows = jnp.ones((DENOM_ROWS, 2 * w), BF16)

    def band(n):
        tile, half = divmod(n, T_DIFF // w)
        if n == 0:
            k_band = jnp.concatenate([kprev_ref[0], kcur_ref[0, 0:w, :]], axis=0)
            vt_band = jnp.concatenate([vtprev_ref[0, 0, :, w:2 * w], vtcur_ref[0, 0, :, 0:w]], axis=1)
            return k_band, vt_band, bias_ref[jnp.where(i == 0, 1, 0)]
        k_band = kcur_ref[0, (n - 1) * w:(n + 1) * w, :]
        if half == 1:
            vt_band = vtcur_ref[0, tile]
        else:
            vt_band = jnp.concatenate([vtcur_ref[0, tile - 1, :, w:2 * w], vtcur_ref[0, tile, :, 0:w]], axis=1)
        return k_band, vt_band, bias_ref[0]

    def scores(n, g):
        tile, half = divmod(n, T_DIFF // w)
        k_band, _, bias = band(n)
        qg = qt_ref[0, tile, g * LANES:(g + 1) * LANES, half * w:(half + 1) * w]
        zero = jnp.zeros_like(qg)
        qbd = jnp.concatenate([jnp.where(kv0_feat, qg, zero), jnp.where(kv0_feat, zero, qg)], axis=1)
        s = jnp.dot(k_band, qbd, preferred_element_type=F32)
        s_ref[n * SWA_GROUP + g] = s + bias

    def sink_row(g):
        return jnp.where(first_cols, sink_ref[g], sink_ref[SWA_GROUP + g]) * LOG2E

    def attend(n, g):
        c = n * SWA_GROUP + g
        _, vt_band, _ = band(n)
        v_aug = jnp.concatenate([vt_band, ones_rows], axis=0)
        s = s_ref[c]
        m = jnp.maximum(jnp.max(s, axis=0, keepdims=True), sink_row(g))
        p = jnp.exp2(s - m)
        pv_ref[c] = jnp.dot(v_aug, p.astype(BF16), preferred_element_type=F32)
        m_ref[c] = m

    def finish(n, g):
        c = n * SWA_GROUP + g
        pv = pv_ref[c]
        inv = 1.0 / (pv[LANES:LANES + 1, :] + jnp.exp2(sink_row(g) - m_ref[c]))
        o_pair = jnp.concatenate([pv[0:HEAD_DIM, 0:w] * inv[:, 0:w],
                                  pv[HEAD_DIM:LANES, w:2 * w] * inv[:, w:2 * w]], axis=0)
        o_ref[0, n * w:(n + 1) * w, g * LANES:(g + 1) * LANES] = o_pair.T.astype(BF16)

    order = [(n, g) for n in range(n_sub) for g in range(SWA_GROUP)]
    stages = (scores, attend, finish)
    group = SWA_PIPE_GROUP
    groups = [order[k:k + group] for k in range(0, len(order), group)]
    for tick in range(len(groups) + len(stages) - 1):
        for lag, stage in enumerate(stages):
            if 0 <= tick - lag < len(groups):
                for chain in groups[tick - lag]:
                    stage(*chain)


def _swa_call(proj_a, proj_t, sinks):
    n_q = SEQ // TQ_SWA
    tiles = TQ_SWA // T_DIFF
    subs = TQ_SWA // WINDOW
    return pl.pallas_call(
        _swa_kernel,
        grid=(BATCH, n_q),
        in_specs=[
            pl.BlockSpec(memory_space=pltpu.SMEM),
            pl.BlockSpec((1, tiles, SWA_WIDTH, T_DIFF), lambda b, i: (b, i, T_Q_SWA // SWA_WIDTH, 0)),
            pl.BlockSpec((1, WINDOW, LANES), lambda b, i: (b, jnp.maximum(i * subs - 1, 0), A_K_SWA // LANES)),
            pl.BlockSpec((1, TQ_SWA, LANES), lambda b, i: (b, i, A_K_SWA // LANES)),
            pl.BlockSpec((1, 1, LANES, T_DIFF), lambda b, i: (b, jnp.maximum(i * tiles - 1, 0), T_V_SWA // LANES, 0)),
            pl.BlockSpec((1, tiles, LANES, T_DIFF), lambda b, i: (b, i, T_V_SWA // LANES, 0)),
        ],
        out_specs=pl.BlockSpec((1, TQ_SWA, SWA_WIDTH), lambda b, i: (b, i, 0)),
        out_shape=jax.ShapeDtypeStruct((BATCH, SEQ, SWA_WIDTH), BF16),
        scratch_shapes=[
            pltpu.VMEM((subs * SWA_GROUP, 2 * WINDOW, 2 * WINDOW), F32),
            pltpu.VMEM((subs * SWA_GROUP, LANES + DENOM_ROWS, 2 * WINDOW), F32),
            pltpu.VMEM((subs * SWA_GROUP, 1, 2 * WINDOW), F32),
            pltpu.VMEM((2, 2 * WINDOW, 2 * WINDOW), F32),
        ],
        compiler_params=pltpu.CompilerParams(
            dimension_semantics=("arbitrary", "arbitrary"),
            vmem_limit_bytes=VMEM_LIMIT_BYTES),
        name="swa_attn",
    )(sinks, proj_t, proj_a, proj_a, proj_t, proj_t)


def _post_kernel(x_ref, od_ref, os_ref, gain_ref, mod_ref, wo_ref, wg_ref, wu_ref, wd_ref, o_ref):
    g_mix_post, g_ffn_pre, g_ffn_post = (_row(gain_ref, k) for k in (GAIN_MIX_POST, GAIN_FFN_PRE, GAIN_FFN_POST))
    gate1, shift2, scale2, gate2 = (_row(mod_ref, k) for k in (MOD_GT1, MOD_SH2, MOD_SC2, MOD_GT2))

    def mixer_residual(rows):
        x = x_ref[0, rows, :]
        y = jnp.dot(od_ref[0, rows, :], wo_ref[0, 0:DIFF_WIDTH, :], preferred_element_type=F32)
        y = y + jnp.dot(os_ref[0, rows, :], wo_ref[0, DIFF_WIDTH:D_MODEL, :], preferred_element_type=F32)
        x1 = x + gate1 * (_rms(y) * g_mix_post)
        h = (_rms(x1) * g_ffn_pre) * (1.0 + scale2) + shift2
        return x1, h.astype(BF16)

    def ffn_residual(x1, hb):
        gate = jnp.dot(hb, wg_ref[0], preferred_element_type=F32)
        up = jnp.dot(hb, wu_ref[0], preferred_element_type=F32)
        act = (gate / (1.0 + jnp.exp(-gate))) * up
        y2 = jnp.dot(act.astype(BF16), wd_ref[0], preferred_element_type=F32)
        return x1 + gate2 * (_rms(y2) * g_ffn_post)

    subs = [slice(k * SUB_POST, (k + 1) * SUB_POST) for k in range(TM_POST // SUB_POST)]
    staged = [mixer_residual(rows) for rows in subs]
    for rows, (x1, hb) in zip(subs, staged):
        o_ref[0, rows, :] = ffn_residual(x1, hb)


def _post_call(layer, x, o_diff, o_swa, mod3, gains, w_out_b, w_gate_b, w_up_b, w_down_b):
    n_t = SEQ // TM_POST
    return pl.pallas_call(
        _post_kernel,
        grid=(BATCH, n_t),
        in_specs=[
            pl.BlockSpec((1, TM_POST, D_MODEL), lambda b, i: (b, i, 0)),
            pl.BlockSpec((1, TM_POST, DIFF_WIDTH), lambda b, i: (b, i, 0)),
            pl.BlockSpec((1, TM_POST, SWA_WIDTH), lambda b, i: (b, i, 0)),
            _gain_spec(layer),
            _mod_spec(layer),
            _layer_weight(layer, (D_MODEL, D_MODEL)),
            _layer_weight(layer, (D_MODEL, D_FF)),
            _layer_weight(layer, (D_MODEL, D_FF)),
            _layer_weight(layer, (D_FF, D_MODEL)),
        ],
        out_specs=pl.BlockSpec((1, TM_POST, D_MODEL), lambda b, i: (b, i, 0)),
        out_shape=jax.ShapeDtypeStruct((BATCH, SEQ, D_MODEL), F32),
        compiler_params=pltpu.CompilerParams(
            dimension_semantics=("arbitrary", "arbitrary"),
            vmem_limit_bytes=VMEM_LIMIT_BYTES),
        name="post_ffn",
    )(x, o_diff, o_swa, gains, mod3, w_out_b, w_gate_b, w_up_b, w_down_b)


def kernel(x, c, ada_w, ada_b, g_mix_pre, g_mix_post, g_ffn_pre, g_ffn_post, w_in, lambda_q1, lambda_k1,
           lambda_q2, lambda_k2, subln_g, sinks, w_out, w_gate, w_up, w_down):
    tables = _rope_tables()

    mod = _ada_call(c, ada_w, ada_b)
    mod3 = mod.reshape(DEPTH * BATCH, 6, D_MODEL)
    gains = jnp.stack([g_mix_pre, g_mix_post, g_ffn_pre, g_ffn_post], axis=1)

    w_a, w_t = _projection_weights(w_in)
    w_out_b = _out_proj_weight(w_out)
    w_gate_b, w_up_b, w_down_b = w_gate.astype(BF16), w_up.astype(BF16), w_down.astype(BF16)

    for layer in range(DEPTH):
        lam_vec = jnp.stack([lambda_q1[layer], lambda_k1[layer], lambda_q2[layer], lambda_k2[layer]])
        proj_a, proj_t = _pre_call(layer, x, mod3, gains, w_a, w_t, tables)
        o_diff = _diff_call(layer, proj_a, proj_t, lam_vec, subln_g[layer])
        o_swa = _swa_call(proj_a, proj_t, sinks[layer])
        x = _post_call(layer, x, o_diff, o_swa, mod3, gains, w_out_b, w_gate_b, w_up_b, w_down_b)
    return x
```

```python
import functools
import math

import numpy as np
import jax
import jax.numpy as jnp
from jax import lax
from jax.experimental import pallas as pl
from jax.experimental.pallas import tpu as pltpu

D_MODEL = 1024
BATCH = 16
SEQ = 2048
DEPTH = 2
HEAD_DIM = 64
HALF = HEAD_DIM // 2
DIFF_HEADS = 4
DIFF_WIDTH = 512
SWA_HEADS = 8
SWA_KV_HEADS = 2
SWA_GROUP = SWA_HEADS // SWA_KV_HEADS
SWA_WIDTH = 512
WINDOW = 128
ROPE_THETA = 10000.0
D_FF = 2816
EPS = 1e-6
NEG = -1e30
LOG2E = math.log2(math.e)

LANES = 128

REF_Q_DIFF, REF_K_DIFF, REF_V_DIFF, REF_Q_SWA, REF_K_SWA, REF_V_SWA = 0, 512, 1024, 1536, 2048, 2176
A_K_DIFF, A_K_SWA, A_WIDTH = 0, 512, 640
T_Q_DIFF, T_V_DIFF, T_Q_SWA, T_V_SWA, T_WIDTH = 0, 512, 1024, 1536, 1664
T_ROPE_ROWS = ((T_Q_DIFF, DIFF_WIDTH), (T_Q_SWA, SWA_WIDTH))

VMEM_LIMIT_BYTES = 56 * 1024 * 1024

TM_PRE = 512
TM_POST = 1024
SUB_POST = 256
T_DIFF = 256
DENOM_ROWS = 16
TQ_SWA = 2048
SWA_PIPE_GROUP = 4

N_MOD = 6
MOD_SH1, MOD_SC1, MOD_GT1, MOD_SH2, MOD_SC2, MOD_GT2 = range(N_MOD)
N_GAIN = 4
GAIN_MIX_PRE, GAIN_MIX_POST, GAIN_FFN_PRE, GAIN_FFN_POST = range(N_GAIN)

F32 = jnp.float32
BF16 = jnp.bfloat16
NT_DIMS = (((1,), (1,)), ((), ()))


def _projection_weights(w_in):
    def section(start, width):
        return w_in[:, :, start:start + width]

    def halves_outer(w, pairs):
        w = w.reshape(DEPTH, D_MODEL, pairs, 2, 2, HALF)
        return jnp.swapaxes(w, 3, 4).reshape(DEPTH, D_MODEL, pairs * LANES)

    score_scale = HEAD_DIM ** -0.5
    q_diff = halves_outer(section(REF_Q_DIFF, DIFF_WIDTH), DIFF_HEADS) * score_scale
    k_diff = halves_outer(section(REF_K_DIFF, DIFF_WIDTH), DIFF_HEADS)
    v_diff = section(REF_V_DIFF, DIFF_WIDTH)
    q_swa = section(REF_Q_SWA, SWA_WIDTH).reshape(DEPTH, D_MODEL, SWA_KV_HEADS, SWA_GROUP, 2, HALF)
    q_swa = q_swa.transpose(0, 1, 3, 4, 2, 5).reshape(DEPTH, D_MODEL, SWA_WIDTH) * score_scale
    k_swa = halves_outer(section(REF_K_SWA, LANES), 1)
    v_swa = section(REF_V_SWA, LANES)
    w_a = jnp.concatenate([k_diff, k_swa], axis=-1).astype(BF16)
    w_t = jnp.swapaxes(jnp.concatenate([q_diff, v_diff, q_swa, v_swa], axis=-1), 1, 2).astype(BF16)
    return w_a, w_t


def _out_proj_weight(w_out):
    diff_rows = w_out[:, 0:DIFF_WIDTH, :]
    swa_rows = w_out[:, DIFF_WIDTH:, :].reshape(DEPTH, SWA_KV_HEADS, SWA_GROUP, HEAD_DIM, D_MODEL)
    swa_rows = jnp.swapaxes(swa_rows, 1, 2).reshape(DEPTH, SWA_WIDTH, D_MODEL)
    return jnp.concatenate([diff_rows, swa_rows], axis=1).astype(BF16)


def _rope_tables():
    pos = jnp.arange(SEQ, dtype=F32)
    inv = ROPE_THETA ** (-jnp.arange(0, HEAD_DIM, 2, dtype=F32) / HEAD_DIM)
    ang = pos[:, None] * inv[None, :]
    cos, sin = jnp.cos(ang), jnp.sin(ang)
    cos4 = jnp.concatenate([cos, cos, cos, cos], axis=-1)
    sin4 = jnp.concatenate([-sin, -sin, sin, sin], axis=-1)
    cos_t = jnp.concatenate([cos, cos], axis=-1).T * LOG2E
    sin_t = jnp.concatenate([sin, sin], axis=-1).T * LOG2E
    return cos4, sin4, cos_t, sin_t


def _rms(x):
    return x * lax.rsqrt(jnp.mean(x * x, axis=-1, keepdims=True) + EPS)


def _ada_kernel(c_ref, w_ref, b_ref, o_ref):
    c = c_ref[...]
    c_act = c / (1.0 + jnp.exp(-c))
    o_ref[0] = jnp.dot(c_act, w_ref[0], preferred_element_type=F32) + b_ref[0]


def _ada_call(c, ada_w, ada_b):
    return pl.pallas_call(
        _ada_kernel,
        grid=(DEPTH, N_MOD),
        in_specs=[
            pl.BlockSpec((BATCH, D_MODEL), lambda l, k: (0, 0)),
            pl.BlockSpec((1, D_MODEL, D_MODEL), lambda l, k: (l, 0, k)),
            pl.BlockSpec((1, 1, D_MODEL), lambda l, k: (l, 0, k)),
        ],
        out_specs=pl.BlockSpec((1, BATCH, D_MODEL), lambda l, k: (l, 0, k)),
        out_shape=jax.ShapeDtypeStruct((DEPTH, BATCH, N_MOD * D_MODEL), F32),
        name="ada_mod",
    )(c, ada_w, ada_b.reshape(DEPTH, 1, N_MOD * D_MODEL))


def _pre_kernel(x_ref, gain_ref, mod_ref, wa_ref, wt_ref, cos_ref, sin_ref, cost_ref, sint_ref,
                oa_ref, ot_ref):
    gain = _row(gain_ref, GAIN_MIX_PRE)
    shift, scale = _row(mod_ref, MOD_SH1), _row(mod_ref, MOD_SC1)

    def modulate(tok):
        x = x_ref[0, tok, :]
        h = (_rms(x) * gain) * (1.0 + scale) + shift
        return h.astype(BF16)

    def project(hb):
        return (jnp.dot(hb, wa_ref[0], preferred_element_type=F32),
                lax.dot_general(wt_ref[0], hb, NT_DIMS, preferred_element_type=F32))

    def rotate_store(blk, tok, proj, proj_t):
        cos = cos_ref[tok, :]
        sin = sin_ref[tok, :]
        for grp in range(A_WIDTH // LANES):
            cols = slice(grp * LANES, (grp + 1) * LANES)
            xg = proj[:, cols]
            oa_ref[0, tok, cols] = (xg * cos + pltpu.roll(xg, HEAD_DIM, 1) * sin).astype(BF16)

        cos_t = cost_ref[:, tok]
        sin_t = sint_ref[:, tok]
        for start, size in T_ROPE_ROWS:
            for grp in range(size // LANES):
                base = start + grp * LANES
                mid = base + HEAD_DIM
                lo = proj_t[base:mid]
                hi = proj_t[mid:base + LANES]
                ot_ref[0, blk, base:mid, :] = (lo * cos_t - hi * sin_t).astype(BF16)
                ot_ref[0, blk, mid:base + LANES, :] = (hi * cos_t + lo * sin_t).astype(BF16)
        for start, size in ((T_V_DIFF, DIFF_WIDTH), (T_V_SWA, LANES)):
            ot_ref[0, blk, start:start + size, :] = proj_t[start:start + size].astype(BF16)

    toks = [slice(blk * T_DIFF, (blk + 1) * T_DIFF) for blk in range(TM_PRE // T_DIFF)]
    staged = [modulate(tok) for tok in toks]
    projected = [project(hb) for hb in staged]
    for blk, (tok, (proj, proj_t)) in enumerate(zip(toks, projected)):
        rotate_store(blk, tok, proj, proj_t)


def _mod_spec(layer):
    return pl.BlockSpec((1, N_MOD, D_MODEL), lambda b, i: (layer * BATCH + b, 0, 0))


def _gain_spec(layer):
    return pl.BlockSpec((1, N_GAIN, D_MODEL), lambda b, i: (layer, 0, 0))


def _row(ref, k):
    return ref[0, k:k + 1, :]


def _layer_weight(layer, shape):
    return pl.BlockSpec((1,) + shape, lambda b, i: (layer,) + (0,) * len(shape), pipeline_mode=pl.Buffered(1))


def _pre_call(layer, x, mod3, gains, w_a, w_t, tables):
    cos4, sin4, cos_t, sin_t = tables
    n_t = SEQ // TM_PRE
    n_blk = TM_PRE // T_DIFF
    return pl.pallas_call(
        _pre_kernel,
        grid=(BATCH, n_t),
        in_specs=[
            pl.BlockSpec((1, TM_PRE, D_MODEL), lambda b, i: (b, i, 0)),
            _gain_spec(layer),
            _mod_spec(layer),
            _layer_weight(layer, (D_MODEL, A_WIDTH)),
            _layer_weight(layer, (T_WIDTH, D_MODEL)),
            pl.BlockSpec((TM_PRE, LANES), lambda b, i: (i, 0)),
            pl.BlockSpec((TM_PRE, LANES), lambda b, i: (i, 0)),
            pl.BlockSpec((HEAD_DIM, TM_PRE), lambda b, i: (0, i)),
            pl.BlockSpec((HEAD_DIM, TM_PRE), lambda b, i: (0, i)),
        ],
        out_specs=[
            pl.BlockSpec((1, TM_PRE, A_WIDTH), lambda b, i: (b, i, 0)),
            pl.BlockSpec((1, n_blk, T_WIDTH, T_DIFF), lambda b, i: (b, i, 0, 0)),
        ],
        out_shape=[
            jax.ShapeDtypeStruct((BATCH, SEQ, A_WIDTH), BF16),
            jax.ShapeDtypeStruct((BATCH, SEQ // T_DIFF, T_WIDTH, T_DIFF), BF16),
        ],
        compiler_params=pltpu.CompilerParams(
            dimension_semantics=("arbitrary", "arbitrary"),
            vmem_limit_bytes=VMEM_LIMIT_BYTES),
        name="pre_mixer",
    )(x, gains, mod3, w_a, w_t, cos4, sin4, cos_t, sin_t)


def _diff_kernel(lam_ref, g_ref, qt_ref, k_ref, vt_ref, o_ref,
                 qbd_ref, m_ref, acc_ref, s0_ref, s1_ref, max0_ref, max1_ref, *, lam_init):
    t = T_DIFF
    feat = lax.broadcasted_iota(jnp.int32, (LANES, 1), 0)
    first_map = ((feat // HALF) % 2) == 0

    chains = [(sub, h) for sub in range(2) for h in range(DIFF_HEADS)]
    sub_chains = lambda sub: [c for c, (sb, _) in enumerate(chains) if sb == sub]
    all_chains = list(range(len(chains)))
    ones_rows = jnp.ones((DENOM_ROWS, t), BF16)

    lam_vec = lam_ref[...]
    dot1 = jnp.sum(lam_vec[0:1] * lam_vec[1:2], axis=-1, keepdims=True)
    dot2 = jnp.sum(lam_vec[2:3] * lam_vec[3:4], axis=-1, keepdims=True)
    lam = jnp.exp(dot1) - jnp.exp(dot2) + lam_init
    g = g_ref[...]

    def scores(j, c, s_ref, max_ref):
        h = chains[c][1]
        start = pl.multiple_of(j * t, t)
        kh = k_ref[0, pl.ds(start, t), h * LANES:(h + 1) * LANES]
        s = jnp.dot(kh, qbd_ref[c], preferred_element_type=F32)
        s_ref[c] = s
        max_ref[c] = jnp.max(s, axis=0, keepdims=True)

    def accumulate(j, c, s_ref, max_ref, diagonal):
        h = chains[c][1]
        vth = vt_ref[0, j, h * LANES:(h + 1) * LANES, :]
        s = s_ref[c]
        if diagonal:
            key = lax.broadcasted_iota(jnp.int32, (t, 2 * t), 0)
            qry = lax.broadcasted_iota(jnp.int32, (t, 2 * t), 1) & (t - 1)
            s = jnp.where(key <= qry, s, NEG)
            m_blk = jnp.max(s, axis=0, keepdims=True)
        else:
            m_blk = max_ref[c]
        m_old = m_ref[c]
        m_new = jnp.maximum(m_old, m_blk)
        p = jnp.exp2(s - m_new)
        alpha = jnp.exp2(m_old - m_new)
        v_aug = jnp.concatenate([vth, ones_rows], axis=0)
        acc_ref[c] = alpha * acc_ref[c] + jnp.dot(v_aug, p.astype(BF16), preferred_element_type=F32)
        m_ref[c] = m_new

    def sweep_pair(pair_idx, carry):
        diag0 = 2 * pair_idx
        for c, (sub, h) in enumerate(chains):
            qh = qt_ref[0, diag0 + sub, h * LANES:(h + 1) * LANES, :]
            zero = jnp.zeros_like(qh)
            qbd_ref[c, :, 0:t] = jnp.where(first_map, qh, zero)
            qbd_ref[c, :, t:2 * t] = jnp.where(first_map, zero, qh)
            m_ref[c] = jnp.full((1, 2 * t), NEG, F32)
            acc_ref[c] = jnp.zeros((LANES + DENOM_ROWS, 2 * t), F32)

        for c in all_chains:
            scores(0, c, s0_ref, max0_ref)

        def two_blocks(step, inner):
            j = 2 * step
            for c in all_chains:
                scores(j + 1, c, s1_ref, max1_ref)
                accumulate(j, c, s0_ref, max0_ref, False)
            for c in all_chains:
                scores(j + 2, c, s0_ref, max0_ref)
                accumulate(j + 1, c, s1_ref, max1_ref, False)
            return inner

        lax.fori_loop(0, pair_idx, two_blocks, 0)

        def finalize(c):
            sub, h = chains[c]
            denom = acc_ref[c, LANES:LANES + 1, :]
            o_all = acc_ref[c, 0:LANES, :] * (1.0 / denom)
            o = o_all[:, 0:t] - lam * o_all[:, t:2 * t]
            o = o * lax.rsqrt(jnp.mean(o * o, axis=0, keepdims=True) + EPS)
            o = (o * g) * (1.0 - lam_init)
            rows = pl.ds(pl.multiple_of((diag0 + sub) * t, t), t)
            o_ref[0, rows, h * LANES:(h + 1) * LANES] = o.T.astype(BF16)

        for c0, c1 in zip(sub_chains(0), sub_chains(1)):
            scores(diag0 + 1, c1, s1_ref, max1_ref)
            accumulate(diag0, c0, s0_ref, max0_ref, True)
            accumulate(diag0, c1, s0_ref, max0_ref, False)
        for c0, c1 in zip(sub_chains(0), sub_chains(1)):
            accumulate(diag0 + 1, c1, s1_ref, max1_ref, True)
            finalize(c0)
        for c1 in sub_chains(1):
            finalize(c1)
        return carry

    lax.fori_loop(0, SEQ // (2 * t), sweep_pair, 0)


def _diff_call(layer, proj_a, proj_t, lam_vec, subln_g):
    lam_init = 0.8 - 0.6 * math.exp(-0.3 * layer)
    n_blocks = SEQ // T_DIFF
    n_chains = 2 * DIFF_HEADS
    kernel = functools.partial(_diff_kernel, lam_init=lam_init)
    return pl.pallas_call(
        kernel,
        grid=(BATCH,),
        in_specs=[
            pl.BlockSpec((4, HEAD_DIM), lambda b: (0, 0)),
            pl.BlockSpec((LANES, 1), lambda b: (0, 0)),
            pl.BlockSpec((1, n_blocks, DIFF_WIDTH, T_DIFF), lambda b: (b, 0, T_Q_DIFF // DIFF_WIDTH, 0)),
            pl.BlockSpec((1, SEQ, DIFF_WIDTH), lambda b: (b, 0, A_K_DIFF // DIFF_WIDTH)),
            pl.BlockSpec((1, n_blocks, DIFF_WIDTH, T_DIFF), lambda b: (b, 0, T_V_DIFF // DIFF_WIDTH, 0)),
        ],
        out_specs=pl.BlockSpec((1, SEQ, DIFF_WIDTH), lambda b: (b, 0, 0)),
        out_shape=jax.ShapeDtypeStruct((BATCH, SEQ, DIFF_WIDTH), BF16),
        scratch_shapes=[
            pltpu.VMEM((n_chains, LANES, 2 * T_DIFF), BF16),
            pltpu.VMEM((n_chains, 1, 2 * T_DIFF), F32),
            pltpu.VMEM((n_chains, LANES + DENOM_ROWS, 2 * T_DIFF), F32),
            pltpu.VMEM((n_chains, T_DIFF, 2 * T_DIFF), F32),
            pltpu.VMEM((n_chains, T_DIFF, 2 * T_DIFF), F32),
            pltpu.VMEM((n_chains, 1, 2 * T_DIFF), F32),
            pltpu.VMEM((n_chains, 1, 2 * T_DIFF), F32),
        ],
        compiler_params=pltpu.CompilerParams(
            dimension_semantics=("arbitrary",),
            vmem_limit_bytes=VMEM_LIMIT_BYTES),
        name="diff_attn",
    )(lam_vec, subln_g.reshape(LANES, 1), proj_t, proj_a, proj_t)


def _swa_kernel(sink_ref, qt_ref, kprev_ref, kcur_ref, vtprev_ref, vtcur_ref, o_ref,
                s_ref, pv_ref, m_ref, bias_ref):
    i = pl.program_id(1)
    w = WINDOW
    n_sub = TQ_SWA // w
    feat = lax.broadcasted_iota(jnp.int32, (LANES, 1), 0)
    kv0_feat = ((feat // HALF) % 2) == 0
    col = lax.broadcasted_iota(jnp.int32, (1, 2 * w), 1)
    first_cols = col < w

    @pl.when(i == 0)
    def _():
        key = lax.broadcasted_iota(jnp.int32, (2 * w, 2 * w), 0)
        qry = lax.broadcasted_iota(jnp.int32, (2 * w, 2 * w), 1) & (w - 1)
        in_window = (key > qry) & (key <= qry + w)
        bias_ref[0] = jnp.where(in_window, 0.0, NEG)
        bias_ref[1] = jnp.where(in_window & (key >= w), 0.0, NEG)
    ones_rows = jnp.ones((DENOM_ROWS, 2 * w), BF16)

    def band(n):
        tile, half = divmod(n, T_DIFF // w)
        if n == 0:
            k_band = jnp.concatenate([kprev_ref[0], kcur_ref[0, 0:w, :]], axis=0)
            vt_band = jnp.concatenate([vtprev_ref[0, 0, :, w:2 * w], vtcur_ref[0, 0, :, 0:w]], axis=1)
            return k_band, vt_band, bias_ref[jnp.where(i == 0, 1, 0)]
        k_band = kcur_ref[0, (n - 1) * w:(n + 1) * w, :]
        if half == 1:
            vt_band = vtcur_ref[0, tile]
        else:
            vt_band = jnp.concatenate([vtcur_ref[0, tile - 1, :, w:2 * w], vtcur_ref[0, tile, :, 0:w]], axis=1)
        return k_band, vt_band, bias_ref[0]

    def scores(n, g):
        tile, half = divmod(n, T_DIFF // w)
        k_band, _, bias = band(n)
        qg = qt_ref[0, tile, g * LANES:(g + 1) * LANES, half * w:(half + 1) * w]
        zero = jnp.zeros_like(qg)
        qbd = jnp.concatenate([jnp.where(kv0_feat, qg, zero), jnp.where(kv0_feat, zero, qg)], axis=1)
        s = jnp.dot(k_band, qbd, preferred_element_type=F32)
        s_ref[n * SWA_GROUP + g] = s + bias

    def sink_row(g):
        return jnp.where(first_cols, sink_ref[g], sink_ref[SWA_GROUP + g]) * LOG2E

    def attend(n, g):
        c = n * SWA_GROUP + g
        _, vt_band, _ = band(n)
        v_aug = jnp.concatenate([vt_band, ones_rows], axis=0)
        s = s_ref[c]
        m = jnp.maximum(jnp.max(s, axis=0, keepdims=True), sink_row(g))
        p = jnp.exp2(s - m)
        pv_ref[c] = jnp.dot(v_aug, p.astype(BF16), preferred_element_type=F32)
        m_ref[c] = m

    def finish(n, g):
        c = n * SWA_GROUP + g
        pv = pv_ref[c]
        inv = 1.0 / (pv[LANES:LANES + 1, :] + jnp.exp2(sink_row(g) - m_ref[c]))
        o_pair = jnp.concatenate([pv[0:HEAD_DIM, 0:w] * inv[:, 0:w],
                                  pv[HEAD_DIM:LANES, w:2 * w] * inv[:, w:2 * w]], axis=0)
        o_ref[0, n * w:(n + 1) * w, g * LANES:(g + 1) * LANES] = o_pair.T.astype(BF16)

    order = [(n, g) for n in range(n_sub) for g in range(SWA_GROUP)]
    stages = (scores, attend, finish)
    group = SWA_PIPE_GROUP
    groups = [order[k:k + group] for k in range(0, len(order), group)]
    for tick in range(len(groups) + len(stages) - 1):
        for lag, stage in enumerate(stages):
            if 0 <= tick - lag < len(groups):
                for chain in groups[tick - lag]:
                    stage(*chain)


def _swa_call(proj_a, proj_t, sinks):
    n_q = SEQ // TQ_SWA
    tiles = TQ_SWA // T_DIFF
    subs = TQ_SWA // WINDOW
    return pl.pallas_call(
        _swa_kernel,
        grid=(BATCH, n_q),
        in_specs=[
            pl.BlockSpec(memory_space=pltpu.SMEM),
            pl.BlockSpec((1, tiles, SWA_WIDTH, T_DIFF), lambda b, i: (b, i, T_Q_SWA // SWA_WIDTH, 0)),
            pl.BlockSpec((1, WINDOW, LANES), lambda b, i: (b, jnp.maximum(i * subs - 1, 0), A_K_SWA // LANES)),
            pl.BlockSpec((1, TQ_SWA, LANES), lambda b, i: (b, i, A_K_SWA // LANES)),
            pl.BlockSpec((1, 1, LANES, T_DIFF), lambda b, i: (b, jnp.maximum(i * tiles - 1, 0), T_V_SWA // LANES, 0)),
            pl.BlockSpec((1, tiles, LANES, T_DIFF), lambda b, i: (b, i, T_V_SWA // LANES, 0)),
        ],
        out_specs=pl.BlockSpec((1, TQ_SWA, SWA_WIDTH), lambda b, i: (b, i, 0)),
        out_shape=jax.ShapeDtypeStruct((BATCH, SEQ, SWA_WIDTH), BF16),
        scratch_shapes=[
            pltpu.VMEM((subs * SWA_GROUP, 2 * WINDOW, 2 * WINDOW), F32),
            pltpu.VMEM((subs * SWA_GROUP, LANES + DENOM_ROWS, 2 * WINDOW), F32),
            pltpu.VMEM((subs * SWA_GROUP, 1, 2 * WINDOW), F32),
            pltpu.VMEM((2, 2 * WINDOW, 2 * WINDOW), F32),
        ],
        compiler_params=pltpu.CompilerParams(
            dimension_semantics=("arbitrary", "arbitrary"),
            vmem_limit_bytes=VMEM_LIMIT_BYTES),
        name="swa_attn",
    )(sinks, proj_t, proj_a, proj_a, proj_t, proj_t)


def _post_kernel(x_ref, od_ref, os_ref, gain_ref, mod_ref, wo_ref, wg_ref, wu_ref, wd_ref, o_ref):
    g_mix_post, g_ffn_pre, g_ffn_post = (_row(gain_ref, k) for k in (GAIN_MIX_POST, GAIN_FFN_PRE, GAIN_FFN_POST))
    gate1, shift2, scale2, gate2 = (_row(mod_ref, k) for k in (MOD_GT1, MOD_SH2, MOD_SC2, MOD_GT2))

    def mixer_residual(rows):
        x = x_ref[0, rows, :]
        y = jnp.dot(od_ref[0, rows, :], wo_ref[0, 0:DIFF_WIDTH, :], preferred_element_type=F32)
        y = y + jnp.dot(os_ref[0, rows, :], wo_ref[0, DIFF_WIDTH:D_MODEL, :], preferred_element_type=F32)
        x1 = x + gate1 * (_rms(y) * g_mix_post)
        h = (_rms(x1) * g_ffn_pre) * (1.0 + scale2) + shift2
        return x1, h.astype(BF16)

    def ffn_residual(x1, hb):
        gate = jnp.dot(hb, wg_ref[0], preferred_element_type=F32)
        up = jnp.dot(hb, wu_ref[0], preferred_element_type=F32)
        act = (gate / (1.0 + jnp.exp(-gate))) * up
        y2 = jnp.dot(act.astype(BF16), wd_ref[0], preferred_element_type=F32)
        return x1 + gate2 * (_rms(y2) * g_ffn_post)

    subs = [slice(k * SUB_POST, (k + 1) * SUB_POST) for k in range(TM_POST // SUB_POST)]
    staged = [mixer_residual(rows) for rows in subs]
    for rows, (x1, hb) in zip(subs, staged):
        o_ref[0, rows, :] = ffn_residual(x1, hb)


def _post_call(layer, x, o_diff, o_swa, mod3, gains, w_out_b, w_gate_b, w_up_b, w_down_b):
    n_t = SEQ // TM_POST
    return pl.pallas_call(
        _post_kernel,
        grid=(BATCH, n_t),
        in_specs=[
            pl.BlockSpec((1, TM_POST, D_MODEL), lambda b, i: (b, i, 0)),
            pl.BlockSpec((1, TM_POST, DIFF_WIDTH), lambda b, i: (b, i, 0)),
            pl.BlockSpec((1, TM_POST, SWA_WIDTH), lambda b, i: (b, i, 0)),
            _gain_spec(layer),
            _mod_spec(layer),
            _layer_weight(layer, (D_MODEL, D_MODEL)),
            _layer_weight(layer, (D_MODEL, D_FF)),
            _layer_weight(layer, (D_MODEL, D_FF)),
            _layer_weight(layer, (D_FF, D_MODEL)),
        ],
        out_specs=pl.BlockSpec((1, TM_POST, D_MODEL), lambda b, i: (b, i, 0)),
        out_shape=jax.ShapeDtypeStruct((BATCH, SEQ, D_MODEL), F32),
        compiler_params=pltpu.CompilerParams(
            dimension_semantics=("arbitrary", "arbitrary"),
            vmem_limit_bytes=VMEM_LIMIT_BYTES),
        name="post_ffn",
    )(x, o_diff, o_swa, gains, mod3, w_out_b, w_gate_b, w_up_b, w_down_b)


def kernel(x, c, ada_w, ada_b, g_mix_pre, g_mix_post, g_ffn_pre, g_ffn_post, w_in, lambda_q1, lambda_k1,
           lambda_q2, lambda_k2, subln_g, sinks, w_out, w_gate, w_up, w_down):
    tables = _rope_tables()

    mod = _ada_call(c, ada_w, ada_b)
    mod3 = mod.reshape(DEPTH * BATCH, N_MOD, D_MODEL)
    gains = jnp.stack([g_mix_pre, g_mix_post, g_ffn_pre, g_ffn_post], axis=1)

    w_a, w_t = _projection_weights(w_in)
    w_out_b = _out_proj_weight(w_out)
    w_gate_b, w_up_b, w_down_b = w_gate.astype(BF16), w_up.astype(BF16), w_down.astype(BF16)

    for layer in range(DEPTH):
        lam_vec = jnp.stack([lambda_q1[layer], lambda_k1[layer], lambda_q2[layer], lambda_k2[layer]])
        proj_a, proj_t = _pre_call(layer, x, mod3, gains, w_a, w_t, tables)
        o_diff = _diff_call(layer, proj_a, proj_t, lam_vec, subln_g[layer])
        o_swa = _swa_call(proj_a, proj_t, sinks[layer])
        x = _post_call(layer, x, o_diff, o_swa, mod3, gains, w_out_b, w_gate_b, w_up_b, w_down_b)
    return x
```

```python
import functools
import math

import numpy as np
import jax
import jax.numpy as jnp
from jax import lax
from jax.experimental import pallas as pl
from jax.experimental.pallas import tpu as pltpu

D_MODEL = 1024
BATCH = 16
SEQ = 2048
DEPTH = 2
HEAD_DIM = 64
HALF = HEAD_DIM // 2
DIFF_HEADS = 4
DIFF_WIDTH = 512
SWA_HEADS = 8
SWA_KV_HEADS = 2
SWA_GROUP = SWA_HEADS // SWA_KV_HEADS
SWA_WIDTH = 512
WINDOW = 128
ROPE_THETA = 10000.0
D_FF = 2816
EPS = 1e-6
NEG = -1e30
LOG2E = math.log2(math.e)

LANES = 128

REF_Q_DIFF, REF_K_DIFF, REF_V_DIFF, REF_Q_SWA, REF_K_SWA, REF_V_SWA = 0, 512, 1024, 1536, 2048, 2176
A_K_DIFF, A_K_SWA, A_WIDTH = 0, 512, 640
T_Q_DIFF, T_V_DIFF, T_Q_SWA, T_V_SWA, T_WIDTH = 0, 512, 1024, 1536, 1664
T_ROPE_ROWS = ((T_Q_DIFF, DIFF_WIDTH), (T_Q_SWA, SWA_WIDTH))

VMEM_LIMIT_BYTES = 56 * 1024 * 1024

TM_PRE = 1024
TM_POST = 1024
SUB_POST = 256
T_DIFF = 256
DENOM_ROWS = 16
TQ_SWA = 2048
SWA_PIPE_GROUP = 4

N_MOD = 6
MOD_SH1, MOD_SC1, MOD_GT1, MOD_SH2, MOD_SC2, MOD_GT2 = range(N_MOD)
N_GAIN = 4
GAIN_MIX_PRE, GAIN_MIX_POST, GAIN_FFN_PRE, GAIN_FFN_POST = range(N_GAIN)

F32 = jnp.float32
BF16 = jnp.bfloat16
NT_DIMS = (((1,), (1,)), ((), ()))


def _projection_weights(w_in):
    def section(start, width):
        return w_in[:, :, start:start + width]

    def halves_outer(w, pairs):
        w = w.reshape(DEPTH, D_MODEL, pairs, 2, 2, HALF)
        return jnp.swapaxes(w, 3, 4).reshape(DEPTH, D_MODEL, pairs * LANES)

    score_scale = HEAD_DIM ** -0.5
    q_diff = halves_outer(section(REF_Q_DIFF, DIFF_WIDTH), DIFF_HEADS) * score_scale
    k_diff = halves_outer(section(REF_K_DIFF, DIFF_WIDTH), DIFF_HEADS)
    v_diff = section(REF_V_DIFF, DIFF_WIDTH)
    q_swa = section(REF_Q_SWA, SWA_WIDTH).reshape(DEPTH, D_MODEL, SWA_KV_HEADS, SWA_GROUP, 2, HALF)
    q_swa = q_swa.transpose(0, 1, 3, 4, 2, 5).reshape(DEPTH, D_MODEL, SWA_WIDTH) * score_scale
    k_swa = halves_outer(section(REF_K_SWA, LANES), 1)
    v_swa = section(REF_V_SWA, LANES)
    w_a = jnp.concatenate([k_diff, k_swa], axis=-1).astype(BF16)
    w_t = jnp.swapaxes(jnp.concatenate([q_diff, v_diff, q_swa, v_swa], axis=-1), 1, 2).astype(BF16)
    return w_a, w_t


def _out_proj_weight(w_out):
    diff_rows = w_out[:, 0:DIFF_WIDTH, :]
    swa_rows = w_out[:, DIFF_WIDTH:, :].reshape(DEPTH, SWA_KV_HEADS, SWA_GROUP, HEAD_DIM, D_MODEL)
    swa_rows = jnp.swapaxes(swa_rows, 1, 2).reshape(DEPTH, SWA_WIDTH, D_MODEL)
    return jnp.concatenate([diff_rows, swa_rows], axis=1).astype(BF16)


def _rope_tables():
    pos = jnp.arange(SEQ, dtype=F32)
    inv = ROPE_THETA ** (-jnp.arange(0, HEAD_DIM, 2, dtype=F32) / HEAD_DIM)
    ang = pos[:, None] * inv[None, :]
    cos, sin = jnp.cos(ang), jnp.sin(ang)
    cos4 = jnp.concatenate([cos, cos, cos, cos], axis=-1)
    sin4 = jnp.concatenate([-sin, -sin, sin, sin], axis=-1)
    cos_t = jnp.concatenate([cos, cos], axis=-1).T * LOG2E
    sin_t = jnp.concatenate([sin, sin], axis=-1).T * LOG2E
    return cos4, sin4, cos_t, sin_t


def _rms(x):
    return x * lax.rsqrt(jnp.mean(x * x, axis=-1, keepdims=True) + EPS)


def _ada_kernel(c_ref, w_ref, b_ref, o_ref):
    c = c_ref[...]
    c_act = c / (1.0 + jnp.exp(-c))
    o_ref[0] = jnp.dot(c_act, w_ref[0], preferred_element_type=F32) + b_ref[0]


def _ada_call(c, ada_w, ada_b):
    return pl.pallas_call(
        _ada_kernel,
        grid=(DEPTH, N_MOD),
        in_specs=[
            pl.BlockSpec((BATCH, D_MODEL), lambda l, k: (0, 0)),
            pl.BlockSpec((1, D_MODEL, D_MODEL), lambda l, k: (l, 0, k)),
            pl.BlockSpec((1, 1, D_MODEL), lambda l, k: (l, 0, k)),
        ],
        out_specs=pl.BlockSpec((1, BATCH, D_MODEL), lambda l, k: (l, 0, k)),
        out_shape=jax.ShapeDtypeStruct((DEPTH, BATCH, N_MOD * D_MODEL), F32),
        name="ada_mod",
    )(c, ada_w, ada_b.reshape(DEPTH, 1, N_MOD * D_MODEL))


def _pre_kernel(x_ref, gain_ref, mod_ref, wa_ref, wt_ref, cos_ref, sin_ref, cost_ref, sint_ref,
                oa_ref, ot_ref):
    gain = _row(gain_ref, GAIN_MIX_PRE)
    shift, scale = _row(mod_ref, MOD_SH1), _row(mod_ref, MOD_SC1)

    def modulate(tok):
        x = x_ref[0, tok, :]
        h = (_rms(x) * gain) * (1.0 + scale) + shift
        return h.astype(BF16)

    def project(hb):
        return (jnp.dot(hb, wa_ref[0], preferred_element_type=F32),
                lax.dot_general(wt_ref[0], hb, NT_DIMS, preferred_element_type=F32))

    def rotate_store(blk, tok, proj, proj_t):
        cos = cos_ref[tok, :]
        sin = sin_ref[tok, :]
        for grp in range(A_WIDTH // LANES):
            cols = slice(grp * LANES, (grp + 1) * LANES)
            xg = proj[:, cols]
            oa_ref[0, tok, cols] = (xg * cos + pltpu.roll(xg, HEAD_DIM, 1) * sin).astype(BF16)

        cos_t = cost_ref[:, tok]
        sin_t = sint_ref[:, tok]
        for start, size in T_ROPE_ROWS:
            for grp in range(size // LANES):
                base = start + grp * LANES
                mid = base + HEAD_DIM
                lo = proj_t[base:mid]
                hi = proj_t[mid:base + LANES]
                ot_ref[0, blk, base:mid, :] = (lo * cos_t - hi * sin_t).astype(BF16)
                ot_ref[0, blk, mid:base + LANES, :] = (hi * cos_t + lo * sin_t).astype(BF16)
        for start, size in ((T_V_DIFF, DIFF_WIDTH), (T_V_SWA, LANES)):
            ot_ref[0, blk, start:start + size, :] = proj_t[start:start + size].astype(BF16)

    toks = [slice(blk * T_DIFF, (blk + 1) * T_DIFF) for blk in range(TM_PRE // T_DIFF)]
    staged = [modulate(tok) for tok in toks]
    projected = [project(hb) for hb in staged]
    for blk, (tok, (proj, proj_t)) in enumerate(zip(toks, projected)):
        rotate_store(blk, tok, proj, proj_t)


def _mod_spec(layer):
    return pl.BlockSpec((1, N_MOD, D_MODEL), lambda b, i: (layer * BATCH + b, 0, 0))


def _gain_spec(layer):
    return pl.BlockSpec((1, N_GAIN, D_MODEL), lambda b, i: (layer, 0, 0))


def _row(ref, k):
    return ref[0, k:k + 1, :]


def _layer_weight(layer, shape):
    return pl.BlockSpec((1,) + shape, lambda b, i: (layer,) + (0,) * len(shape), pipeline_mode=pl.Buffered(1))


def _pre_call(layer, x, mod3, gains, w_a, w_t, tables):
    cos4, sin4, cos_t, sin_t = tables
    n_t = SEQ // TM_PRE
    n_blk = TM_PRE // T_DIFF
    return pl.pallas_call(
        _pre_kernel,
        grid=(BATCH, n_t),
        in_specs=[
            pl.BlockSpec((1, TM_PRE, D_MODEL), lambda b, i: (b, i, 0)),
            _gain_spec(layer),
            _mod_spec(layer),
            _layer_weight(layer, (D_MODEL, A_WIDTH)),
            _layer_weight(layer, (T_WIDTH, D_MODEL)),
            pl.BlockSpec((TM_PRE, LANES), lambda b, i: (i, 0)),
            pl.BlockSpec((TM_PRE, LANES), lambda b, i: (i, 0)),
            pl.BlockSpec((HEAD_DIM, TM_PRE), lambda b, i: (0, i)),
            pl.BlockSpec((HEAD_DIM, TM_PRE), lambda b, i: (0, i)),
        ],
        out_specs=[
            pl.BlockSpec((1, TM_PRE, A_WIDTH), lambda b, i: (b, i, 0)),
            pl.BlockSpec((1, n_blk, T_WIDTH, T_DIFF), lambda b, i: (b, i, 0, 0)),
        ],
        out_shape=[
            jax.ShapeDtypeStruct((BATCH, SEQ, A_WIDTH), BF16),
            jax.ShapeDtypeStruct((BATCH, SEQ // T_DIFF, T_WIDTH, T_DIFF), BF16),
        ],
        compiler_params=pltpu.CompilerParams(
            dimension_semantics=("arbitrary", "arbitrary"),
            vmem_limit_bytes=VMEM_LIMIT_BYTES),
        name="pre_mixer",
    )(x, gains, mod3, w_a, w_t, cos4, sin4, cos_t, sin_t)


def _diff_kernel(lam_ref, g_ref, qt_ref, k_ref, vt_ref, o_ref,
                 qbd_ref, m_ref, acc_ref, s0_ref, s1_ref, max0_ref, max1_ref, *, lam_init):
    t = T_DIFF
    feat = lax.broadcasted_iota(jnp.int32, (LANES, 1), 0)
    first_map = ((feat // HALF) % 2) == 0

    chains = [(sub, h) for sub in range(2) for h in range(DIFF_HEADS)]
    sub_chains = lambda sub: [c for c, (sb, _) in enumerate(chains) if sb == sub]
    all_chains = list(range(len(chains)))
    ones_rows = jnp.ones((DENOM_ROWS, t), BF16)

    lam_vec = lam_ref[...]
    dot1 = jnp.sum(lam_vec[0:1] * lam_vec[1:2], axis=-1, keepdims=True)
    dot2 = jnp.sum(lam_vec[2:3] * lam_vec[3:4], axis=-1, keepdims=True)
    lam = jnp.exp(dot1) - jnp.exp(dot2) + lam_init
    g = g_ref[...]

    def scores(j, c, s_ref, max_ref):
        h = chains[c][1]
        start = pl.multiple_of(j * t, t)
        kh = k_ref[0, pl.ds(start, t), h * LANES:(h + 1) * LANES]
        s = jnp.dot(kh, qbd_ref[c], preferred_element_type=F32)
        s_ref[c] = s
        max_ref[c] = jnp.max(s, axis=0, keepdims=True)

    def accumulate(j, c, s_ref, max_ref, diagonal):
        h = chains[c][1]
        vth = vt_ref[0, j, h * LANES:(h + 1) * LANES, :]
        s = s_ref[c]
        if diagonal:
            key = lax.broadcasted_iota(jnp.int32, (t, 2 * t), 0)
            qry = lax.broadcasted_iota(jnp.int32, (t, 2 * t), 1) & (t - 1)
            s = jnp.where(key <= qry, s, NEG)
            m_blk = jnp.max(s, axis=0, keepdims=True)
        else:
            m_blk = max_ref[c]
        m_old = m_ref[c]
        m_new = jnp.maximum(m_old, m_blk)
        p = jnp.exp2(s - m_new)
        alpha = jnp.exp2(m_old - m_new)
        v_aug = jnp.concatenate([vth, ones_rows], axis=0)
        acc_ref[c] = alpha * acc_ref[c] + jnp.dot(v_aug, p.astype(BF16), preferred_element_type=F32)
        m_ref[c] = m_new

    def sweep_pair(pair_idx, carry):
        diag0 = 2 * pair_idx
        for c, (sub, h) in enumerate(chains):
            qh = qt_ref[0, diag0 + sub, h * LANES:(h + 1) * LANES, :]
            zero = jnp.zeros_like(qh)
            qbd_ref[c, :, 0:t] = jnp.where(first_map, qh, zero)
            qbd_ref[c, :, t:2 * t] = jnp.where(first_map, zero, qh)
            m_ref[c] = jnp.full((1, 2 * t), NEG, F32)
            acc_ref[c] = jnp.zeros((LANES + DENOM_ROWS, 2 * t), F32)

        for c in all_chains:
            scores(0, c, s0_ref, max0_ref)

        def two_blocks(step, inner):
            j = 2 * step
            for c in all_chains:
                scores(j + 1, c, s1_ref, max1_ref)
                accumulate(j, c, s0_ref, max0_ref, False)
            for c in all_chains:
                scores(j + 2, c, s0_ref, max0_ref)
                accumulate(j + 1, c, s1_ref, max1_ref, False)
            return inner

        lax.fori_loop(0, pair_idx, two_blocks, 0)

        def finalize(c):
            sub, h = chains[c]
            denom = acc_ref[c, LANES:LANES + 1, :]
            o_all = acc_ref[c, 0:LANES, :] * (1.0 / denom)
            o = o_all[:, 0:t] - lam * o_all[:, t:2 * t]
            o = o * lax.rsqrt(jnp.mean(o * o, axis=0, keepdims=True) + EPS)
            o = (o * g) * (1.0 - lam_init)
            rows = pl.ds(pl.multiple_of((diag0 + sub) * t, t), t)
            o_ref[0, rows, h * LANES:(h + 1) * LANES] = o.T.astype(BF16)

        for c0, c1 in zip(sub_chains(0), sub_chains(1)):
            scores(diag0 + 1, c1, s1_ref, max1_ref)
            accumulate(diag0, c0, s0_ref, max0_ref, True)
            accumulate(diag0, c1, s0_ref, max0_ref, False)
        for c0, c1 in zip(sub_chains(0), sub_chains(1)):
            accumulate(diag0 + 1, c1, s1_ref, max1_ref, True)
            finalize(c0)
        for c1 in sub_chains(1):
            finalize(c1)
        return carry

    lax.fori_loop(0, SEQ // (2 * t), sweep_pair, 0)


def _diff_call(layer, proj_a, proj_t, lam_vec, subln_g):
    lam_init = 0.8 - 0.6 * math.exp(-0.3 * layer)
    n_blocks = SEQ // T_DIFF
    n_chains = 2 * DIFF_HEADS
    kernel = functools.partial(_diff_kernel, lam_init=lam_init)
    return pl.pallas_call(
        kernel,
        grid=(BATCH,),
        in_specs=[
            pl.BlockSpec((4, HEAD_DIM), lambda b: (0, 0)),
            pl.BlockSpec((LANES, 1), lambda b: (0, 0)),
            pl.BlockSpec((1, n_blocks, DIFF_WIDTH, T_DIFF), lambda b: (b, 0, T_Q_DIFF // DIFF_WIDTH, 0)),
            pl.BlockSpec((1, SEQ, DIFF_WIDTH), lambda b: (b, 0, A_K_DIFF // DIFF_WIDTH)),
            pl.BlockSpec((1, n_blocks, DIFF_WIDTH, T_DIFF), lambda b: (b, 0, T_V_DIFF // DIFF_WIDTH, 0)),
        ],
        out_specs=pl.BlockSpec((1, SEQ, DIFF_WIDTH), lambda b: (b, 0, 0)),
        out_shape=jax.ShapeDtypeStruct((BATCH, SEQ, DIFF_WIDTH), BF16),
        scratch_shapes=[
            pltpu.VMEM((n_chains, LANES, 2 * T_DIFF), BF16),
            pltpu.VMEM((n_chains, 1, 2 * T_DIFF), F32),
            pltpu.VMEM((n_chains, LANES + DENOM_ROWS, 2 * T_DIFF), F32),
            pltpu.VMEM((n_chains, T_DIFF, 2 * T_DIFF), F32),
            pltpu.VMEM((n_chains, T_DIFF, 2 * T_DIFF), F32),
            pltpu.VMEM((n_chains, 1, 2 * T_DIFF), F32),
            pltpu.VMEM((n_chains, 1, 2 * T_DIFF), F32),
        ],
        compiler_params=pltpu.CompilerParams(
            dimension_semantics=("arbitrary",),
            vmem_limit_bytes=VMEM_LIMIT_BYTES),
        name="diff_attn",
    )(lam_vec, subln_g.reshape(LANES, 1), proj_t, proj_a, proj_t)


def _swa_kernel(sink_ref, qt_ref, kprev_ref, kcur_ref, vtprev_ref, vtcur_ref, o_ref,
                s_ref, pv_ref, m_ref, bias_ref):
    i = pl.program_id(1)
    w = WINDOW
    n_sub = TQ_SWA // w
    feat = lax.broadcasted_iota(jnp.int32, (LANES, 1), 0)
    kv0_feat = ((feat // HALF) % 2) == 0
    col = lax.broadcasted_iota(jnp.int32, (1, 2 * w), 1)
    first_cols = col < w

    @pl.when(i == 0)
    def _():
        key = lax.broadcasted_iota(jnp.int32, (2 * w, 2 * w), 0)
        qry = lax.broadcasted_iota(jnp.int32, (2 * w, 2 * w), 1) & (w - 1)
        in_window = (key > qry) & (key <= qry + w)
        bias_ref[0] = jnp.where(in_window, 0.0, NEG)
        bias_ref[1] = jnp.where(in_window & (key >= w), 0.0, NEG)
    ones_rows = jnp.ones((DENOM_ROWS, 2 * w), BF16)

    def band(n):
        tile, half = divmod(n, T_DIFF // w)
        if n == 0:
            k_band = jnp.concatenate([kprev_ref[0], kcur_ref[0, 0:w, :]], axis=0)
            vt_band = jnp.concatenate([vtprev_ref[0, 0, :, w:2 * w], vtcur_ref[0, 0, :, 0:w]], axis=1)
            return k_band, vt_band, bias_ref[jnp.where(i == 0, 1, 0)]
        k_band = kcur_ref[0, (n - 1) * w:(n + 1) * w, :]
        if half == 1:
            vt_band = vtcur_ref[0, tile]
        else:
            vt_band = jnp.concatenate([vtcur_ref[0, tile - 1, :, w:2 * w], vtcur_ref[0, tile, :, 0:w]], axis=1)
        return k_band, vt_band, bias_ref[0]

    def scores(n, g):
        tile, half = divmod(n, T_DIFF // w)
        k_band, _, bias = band(n)
        qg = qt_ref[0, tile, g * LANES:(g + 1) * LANES, half * w:(half + 1) * w]
        zero = jnp.zeros_like(qg)
        qbd = jnp.concatenate([jnp.where(kv0_feat, qg, zero), jnp.where(kv0_feat, zero, qg)], axis=1)
        s = jnp.dot(k_band, qbd, preferred_element_type=F32)
        s_ref[n * SWA_GROUP + g] = s + bias

    def sink_row(g):
        return jnp.where(first_cols, sink_ref[g], sink_ref[SWA_GROUP + g]) * LOG2E

    def attend(n, g):
        c = n * SWA_GROUP + g
        _, vt_band, _ = band(n)
        v_aug = jnp.concatenate([vt_band, ones_rows], axis=0)
        s = s_ref[c]
        m = jnp.maximum(jnp.max(s, axis=0, keepdims=True), sink_row(g))
        p = jnp.exp2(s - m)
        pv_ref[c] = jnp.dot(v_aug, p.astype(BF16), preferred_element_type=F32)
        m_ref[c] = m

    def finish(n, g):
        c = n * SWA_GROUP + g
        pv = pv_ref[c]
        inv = 1.0 / (pv[LANES:LANES + 1, :] + jnp.exp2(sink_row(g) - m_ref[c]))
        o_pair = jnp.concatenate([pv[0:HEAD_DIM, 0:w] * inv[:, 0:w],
                                  pv[HEAD_DIM:LANES, w:2 * w] * inv[:, w:2 * w]], axis=0)
        o_ref[0, n * w:(n + 1) * w, g * LANES:(g + 1) * LANES] = o_pair.T.astype(BF16)

    order = [(n, g) for n in range(n_sub) for g in range(SWA_GROUP)]
    stages = (scores, attend, finish)
    group = SWA_PIPE_GROUP
    groups = [order[k:k + group] for k in range(0, len(order), group)]
    for tick in range(len(groups) + len(stages) - 1):
        for lag, stage in enumerate(stages):
            if 0 <= tick - lag < len(groups):
                for chain in groups[tick - lag]:
                    stage(*chain)


def _swa_call(proj_a, proj_t, sinks):
    n_q = SEQ // TQ_SWA
    tiles = TQ_SWA // T_DIFF
    subs = TQ_SWA // WINDOW
    return pl.pallas_call(
        _swa_kernel,
        grid=(BATCH, n_q),
        in_specs=[
            pl.BlockSpec(memory_space=pltpu.SMEM),
            pl.BlockSpec((1, tiles, SWA_WIDTH, T_DIFF), lambda b, i: (b, i, T_Q_SWA // SWA_WIDTH, 0)),
            pl.BlockSpec((1, WINDOW, LANES), lambda b, i: (b, jnp.maximum(i * subs - 1, 0), A_K_SWA // LANES)),
            pl.BlockSpec((1, TQ_SWA, LANES), lambda b, i: (b, i, A_K_SWA // LANES)),
            pl.BlockSpec((1, 1, LANES, T_DIFF), lambda b, i: (b, jnp.maximum(i * tiles - 1, 0), T_V_SWA // LANES, 0)),
            pl.BlockSpec((1, tiles, LANES, T_DIFF), lambda b, i: (b, i, T_V_SWA // LANES, 0)),
        ],
        out_specs=pl.BlockSpec((1, TQ_SWA, SWA_WIDTH), lambda b, i: (b, i, 0)),
        out_shape=jax.ShapeDtypeStruct((BATCH, SEQ, SWA_WIDTH), BF16),
        scratch_shapes=[
            pltpu.VMEM((subs * SWA_GROUP, 2 * WINDOW, 2 * WINDOW), F32),
            pltpu.VMEM((subs * SWA_GROUP, LANES + DENOM_ROWS, 2 * WINDOW), F32),
            pltpu.VMEM((subs * SWA_GROUP, 1, 2 * WINDOW), F32),
            pltpu.VMEM((2, 2 * WINDOW, 2 * WINDOW), F32),
        ],
        compiler_params=pltpu.CompilerParams(
            dimension_semantics=("arbitrary", "arbitrary"),
            vmem_limit_bytes=VMEM_LIMIT_BYTES),
        name="swa_attn",
    )(sinks, proj_t, proj_a, proj_a, proj_t, proj_t)


def _post_kernel(x_ref, od_ref, os_ref, gain_ref, mod_ref, wo_ref, wg_ref, wu_ref, wd_ref, o_ref):
    g_mix_post, g_ffn_pre, g_ffn_post = (_row(gain_ref, k) for k in (GAIN_MIX_POST, GAIN_FFN_PRE, GAIN_FFN_POST))
    gate1, shift2, scale2, gate2 = (_row(mod_ref, k) for k in (MOD_GT1, MOD_SH2, MOD_SC2, MOD_GT2))

    def mixer_residual(rows):
        x = x_ref[0, rows, :]
        y = jnp.dot(od_ref[0, rows, :], wo_ref[0, 0:DIFF_WIDTH, :], preferred_element_type=F32)
        y = y + jnp.dot(os_ref[0, rows, :], wo_ref[0, DIFF_WIDTH:D_MODEL, :], preferred_element_type=F32)
        x1 = x + gate1 * (_rms(y) * g_mix_post)
        h = (_rms(x1) * g_ffn_pre) * (1.0 + scale2) + shift2
        return x1, h.astype(BF16)

    def ffn_residual(x1, hb):
        gate = jnp.dot(hb, wg_ref[0], preferred_element_type=F32)
        up = jnp.dot(hb, wu_ref[0], preferred_element_type=F32)
        act = (gate / (1.0 + jnp.exp(-gate))) * up
        y2 = jnp.dot(act.astype(BF16), wd_ref[0], preferred_element_type=F32)
        return x1 + gate2 * (_rms(y2) * g_ffn_post)

    subs = [slice(k * SUB_POST, (k + 1) * SUB_POST) for k in range(TM_POST // SUB_POST)]
    staged = [mixer_residual(rows) for rows in subs]
    for rows, (x1, hb) in zip(subs, staged):
        o_ref[0, rows, :] = ffn_residual(x1, hb)


def _post_call(layer, x, o_diff, o_swa, mod3, gains, w_out_b, w_gate_b, w_up_b, w_down_b):
    n_t = SEQ // TM_POST
    return pl.pallas_call(
        _post_kernel,
        grid=(BATCH, n_t),
        in_specs=[
            pl.BlockSpec((1, TM_POST, D_MODEL), lambda b, i: (b, i, 0)),
            pl.BlockSpec((1, TM_POST, DIFF_WIDTH), lambda b, i: (b, i, 0)),
            pl.BlockSpec((1, TM_POST, SWA_WIDTH), lambda b, i: (b, i, 0)),
            _gain_spec(layer),
            _mod_spec(layer),
            _layer_weight(layer, (D_MODEL, D_MODEL)),
            _layer_weight(layer, (D_MODEL, D_FF)),
            _layer_weight(layer, (D_MODEL, D_FF)),
            _layer_weight(layer, (D_FF, D_MODEL)),
        ],
        out_specs=pl.BlockSpec((1, TM_POST, D_MODEL), lambda b, i: (b, i, 0)),
        out_shape=jax.ShapeDtypeStruct((BATCH, SEQ, D_MODEL), F32),
        compiler_params=pltpu.CompilerParams(
            dimension_semantics=("arbitrary", "arbitrary"),
            vmem_limit_bytes=VMEM_LIMIT_BYTES),
        name="post_ffn",
    )(x, o_diff, o_swa, gains, mod3, w_out_b, w_gate_b, w_up_b, w_down_b)


def kernel(x, c, ada_w, ada_b, g_mix_pre, g_mix_post, g_ffn_pre, g_ffn_post, w_in, lambda_q1, lambda_k1,
           lambda_q2, lambda_k2, subln_g, sinks, w_out, w_gate, w_up, w_down):
    tables = _rope_tables()

    mod = _ada_call(c, ada_w, ada_b)
    mod3 = mod.reshape(DEPTH * BATCH, N_MOD, D_MODEL)
    gains = jnp.stack([g_mix_pre, g_mix_post, g_ffn_pre, g_ffn_post], axis=1)

    w_a, w_t = _projection_weights(w_in)
    w_out_b = _out_proj_weight(w_out)
    w_gate_b, w_up_b, w_down_b = w_gate.astype(BF16), w_up.astype(BF16), w_down.astype(BF16)

    for layer in range(DEPTH):
        lam_vec = jnp.stack([lambda_q1[layer], lambda_k1[layer], lambda_q2[layer], lambda_k2[layer]])
        proj_a, proj_t = _pre_call(layer, x, mod3, gains, w_a, w_t, tables)
        o_diff = _diff_call(layer, proj_a, proj_t, lam_vec, subln_g[layer])
        o_swa = _swa_call(proj_a, proj_t, sinks[layer])
        x = _post_call(layer, x, o_diff, o_swa, mod3, gains, w_out_b, w_gate_b, w_up_b, w_down_b)
    return x
```
